```python
import math
import jax
import jax.numpy as jnp
from jax import lax
import numpy as np

D_MODEL = 1024
BATCH = 32
SEQ = 2048
DEPTH = 2

GDN_HEADS = 4
GDN_HEAD_DIM = 128
GDN_WIDTH = GDN_HEADS * GDN_HEAD_DIM
GDN_CHUNK = 64
SHORT_CONV = 4
SSD_HEADS = 8
SSD_HEAD_DIM = 64
SSD_WIDTH = SSD_HEADS * SSD_HEAD_DIM
SSD_GROUPS = 2
SSD_STATE = 128
SSD_CHUNK = 64
SSD_XBC = SSD_WIDTH + 2 * SSD_GROUPS * SSD_STATE
HY_SPLITS = (GDN_WIDTH, GDN_WIDTH, GDN_WIDTH, GDN_WIDTH, GDN_HEADS, GDN_HEADS, SSD_WIDTH, SSD_XBC, SSD_HEADS)
HY_IN = sum(HY_SPLITS)
HY_OUT = GDN_WIDTH + SSD_WIDTH
NSA_HEADS = 16
NSA_KV_GROUPS = 2
NSA_HEAD_DIM = 64
NSA_KV = NSA_KV_GROUPS * NSA_HEAD_DIM
CMP_BLOCK = 32
CMP_STRIDE = 16
CMP_HIDDEN = 256
SEL_BLOCK = 64
SEL_TOPK = 6
WINDOW = 512
Q_BLOCK = 64
NSA_SPLITS = (NSA_HEADS * NSA_HEAD_DIM,) + (NSA_KV,) * 6 + (3 * NSA_HEADS,)
NSA_IN = sum(NSA_SPLITS)
NSA_OUT = NSA_HEADS * NSA_HEAD_DIM
D_FF = 256 * ((8 * D_MODEL // 3 + 255) // 256)
FFN_CONV = 3
EPS = 1e-6
SEL_FORCE = 1e9

kernel_name = 'hybrid_gdn_ssd_nsa_convffn'


def split_cols(t, sizes):
    idx = [int(i) for i in np.cumsum(sizes)[:-1]]
    return jnp.split(t, idx, axis=-1)


def rms_norm(x, w):
    xf = x.astype(jnp.float32)
    y = xf * lax.rsqrt(jnp.mean(xf * xf, axis=-1, keepdims=True) + EPS)
    return (y * w.astype(jnp.float32)).astype(x.dtype)


def l2_normalize(x):
    xf = x.astype(jnp.float32)
    return xf * lax.rsqrt(jnp.sum(xf * xf, axis=-1, keepdims=True) + EPS)


def causal_dwconv(x, w, b=None):
    width, ch = w.shape
    y = lax.conv_general_dilated(x, w[:, None, :].astype(x.dtype), window_strides=(1,), padding=[(width - 1, 0)], dimension_numbers=('NWC', 'WIO', 'NWC'), feature_group_count=ch)
    if b is not None:
        y = y + b.astype(x.dtype)
    return y


def masked_softmax(s, mask, axis=-1):
    s = jnp.where(mask, s, -jnp.inf)
    m = jnp.max(s, axis=axis, keepdims=True)
    m = jnp.where(jnp.isfinite(m), m, 0.0)
    e = jnp.exp(s - m)
    return e / jnp.maximum(jnp.sum(e, axis=axis, keepdims=True), 1e-30)


def gated_delta_rule(q, k, v, g, beta):
    f32 = jnp.float32
    bsz, seq, nh, dk = q.shape
    dv = v.shape[-1]
    C = GDN_CHUNK
    nc = seq // C

    def chunk(t):
        t = t.astype(f32).reshape((bsz, nc, C, nh) + t.shape[3:])
        return jnp.moveaxis(t, 3, 1)

    q, k, v, g, beta = chunk(q), chunk(k), chunk(v), chunk(g), chunk(beta)
    gam = jnp.cumsum(g, axis=-1)
    causal = jnp.tril(jnp.ones((C, C), bool))
    strict = jnp.tril(jnp.ones((C, C), bool), -1)
    decay_mat = jnp.exp(jnp.where(causal, gam[..., :, None] - gam[..., None, :], -jnp.inf))
    kb = k * beta[..., None]
    a_mat = jnp.where(strict, jnp.einsum('bhncd,bhnsd->bhncs', kb, k) * decay_mat, 0.0)
    eye = jnp.eye(C, dtype=f32)
    rhs = jnp.concatenate([v * beta[..., None], kb * jnp.exp(gam)[..., None]], axis=-1)
    sol = lax.linalg.triangular_solve(eye + a_mat, rhs, left_side=True, lower=True, unit_diagonal=True)
    u_v, w_k = sol[..., :dv], sol[..., dv:]
    qk = jnp.einsum('bhncd,bhnsd->bhncs', q, k) * decay_mat
    q_dec = q * jnp.exp(gam)[..., None]
    k_dec = k * jnp.exp(gam[..., -1:] - gam)[..., None]
    g_last = jnp.exp(gam[..., -1])

    def step(state, xs):
        qk_c, qd_c, kd_c, uv_c, wk_c, gl_c = xs
        u = uv_c - jnp.einsum('bhcd,bhdv->bhcv', wk_c, state)
        o = jnp.einsum('bhcd,bhdv->bhcv', qd_c, state) + jnp.einsum('bhcs,bhsv->bhcv', qk_c, u)
        state = state * gl_c[..., None, None] + jnp.einsum('bhcd,bhcv->bhdv', kd_c, u)
        return state, o

    xs = tuple(jnp.moveaxis(t, 2, 0) for t in (qk, q_dec, k_dec, u_v, w_k, g_last))
    s0 = jnp.zeros((bsz, nh, dk, dv), f32)
    _, o = lax.scan(step, s0, xs)
    o = jnp.moveaxis(jnp.moveaxis(o, 0, 2), 1, 3)
    return o.reshape(bsz, seq, nh, dv)


def ssd_chunked(x, dt, a, b_m, c_m):
    f32 = jnp.float32
    bsz, seq, nh, hp = x.shape
    ng, ns = b_m.shape[2], b_m.shape[3]
    hpg = nh // ng
    L = SSD_CHUNK
    nc = seq // L
    xc = (x.astype(f32) * dt[..., None]).reshape(bsz, nc, L, ng, hpg, hp)
    bc = b_m.astype(f32).reshape(bsz, nc, L, ng, ns)
    cc = c_m.astype(f32).reshape(bsz, nc, L, ng, ns)
    acs = jnp.cumsum((dt * a).reshape(bsz, nc, L, ng, hpg), axis=2)
    causal = jnp.tril(jnp.ones((L, L), bool))[:, :, None, None]
    decay_in = jnp.exp(jnp.where(causal, acs[:, :, :, None] - acs[:, :, None, :], -jnp.inf))
    cb = jnp.einsum('bclgn,bcsgn->bclsg', cc, bc)
    y_diag = jnp.einsum('bclsgj,bcsgjp->bclgjp', cb[..., None] * decay_in, xc)
    chunk_states = jnp.einsum('bclgn,bclgj,bclgjp->bcgjpn', bc, jnp.exp(acs[:, :, -1:] - acs), xc)
    chunk_decay = jnp.exp(acs[:, :, -1])

    def step(state, inp):
        st, dec = inp
        return state * dec[..., None, None] + st, state

    h0 = jnp.zeros((bsz, ng, hpg, hp, ns), f32)
    _, h_in = lax.scan(step, h0, (jnp.moveaxis(chunk_states, 1, 0), jnp.moveaxis(chunk_decay, 1, 0)))
    h_in = jnp.moveaxis(h_in, 0, 1)
    y_off = jnp.einsum('bclgn,bcgjpn,bclgj->bclgjp', cc, h_in, jnp.exp(acs))
    return (y_diag + y_off).reshape(bsz, seq, nh, hp)


def hybrid_mixer(h, w_in, gdn_conv, gdn_a_log, gdn_dt_bias, gdn_norm, ssd_conv, ssd_conv_bias, ssd_a_log, ssd_dt_bias, ssd_d, ssd_norm, w_out):
    f32 = jnp.float32
    bsz, seq, _ = h.shape
    q, k, v, z_a, b_a, a_a, z_b, xbc, dt_b = split_cols(h @ w_in, HY_SPLITS)
    qkv = jax.nn.silu(causal_dwconv(jnp.concatenate([q, k, v], axis=-1), gdn_conv))
    q, k, v = jnp.split(qkv, 3, axis=-1)
    hd = (bsz, seq, GDN_HEADS, GDN_HEAD_DIM)
    q = l2_normalize(q.reshape(hd)) * (GDN_HEAD_DIM ** -0.5)
    k = l2_normalize(k.reshape(hd))
    beta = jax.nn.sigmoid(b_a.astype(f32))
    g = -jnp.exp(gdn_a_log.astype(f32)) * jax.nn.softplus(a_a.astype(f32) + gdn_dt_bias.astype(f32))
    o_a = gated_delta_rule(q, k, v.reshape(hd), g, beta).astype(h.dtype)
    o_a = (rms_norm(o_a, gdn_norm) * jax.nn.silu(z_a.reshape(hd))).reshape(bsz, seq, GDN_WIDTH)
    xbc = jax.nn.silu(causal_dwconv(xbc, ssd_conv, ssd_conv_bias))
    xs, b_m, c_m = split_cols(xbc, (SSD_WIDTH, SSD_GROUPS * SSD_STATE, SSD_GROUPS * SSD_STATE))
    xs = xs.reshape(bsz, seq, SSD_HEADS, SSD_HEAD_DIM)
    dt = jax.nn.softplus(dt_b.astype(f32) + ssd_dt_bias.astype(f32))
    a = -jnp.exp(ssd_a_log.astype(f32))
    y = ssd_chunked(xs, dt, a, b_m.reshape(bsz, seq, SSD_GROUPS, SSD_STATE), c_m.reshape(bsz, seq, SSD_GROUPS, SSD_STATE))
    y = y + ssd_d.astype(f32)[:, None] * xs.astype(f32)
    y = y.astype(h.dtype).reshape(bsz, seq, SSD_WIDTH) * jax.nn.silu(z_b)
    gw = SSD_WIDTH // SSD_GROUPS
    y = rms_norm(y.reshape(bsz, seq, SSD_GROUPS, gw), ssd_norm.reshape(SSD_GROUPS, gw)).reshape(bsz, seq, SSD_WIDTH)
    return jnp.concatenate([o_a, y], axis=-1) @ w_out


def nsa_mixer(h, w_in, cmp_pos_k, cmp_w1_k, cmp_w2_k, cmp_pos_v, cmp_w1_v, cmp_w2_v, w_out):
    f32 = jnp.float32
    bsz, seq, _ = h.shape
    H, G, Dh = NSA_HEADS, NSA_KV_GROUPS, NSA_HEAD_DIM
    J = H // G
    q, k_c, v_c, k_s, v_s, k_w, v_w, gates = split_cols(h @ w_in, NSA_SPLITS)
    q = q.reshape(bsz, seq, G, J, Dh) * (Dh ** -0.5)
    gates = jax.nn.sigmoid(gates.astype(f32)).reshape(bsz, seq, 3, G, J)
    slopes = jnp.asarray(2.0 ** (-8.0 * np.arange(1, H + 1) / H), f32).reshape(G, J)

    n_cmp = (seq - CMP_BLOCK) // CMP_STRIDE + 1
    cmp_idx = np.arange(n_cmp)[:, None] * CMP_STRIDE + np.arange(CMP_BLOCK)[None, :]

    def compress(t, pos, w1, w2):
        blk = t.reshape(bsz, seq, G, Dh)[:, cmp_idx] + pos[:, None, :]
        blk = jnp.moveaxis(blk, 3, 2).reshape(bsz, n_cmp, G, CMP_BLOCK * Dh)
        return jax.nn.silu(blk @ w1) @ w2

    k_cmp = compress(k_c, cmp_pos_k, cmp_w1_k, cmp_w2_k).astype(f32)
    v_cmp = compress(v_c, cmp_pos_v, cmp_w1_v, cmp_w2_v).astype(f32)
    cmp_end = jnp.asarray(cmp_idx[:, -1], jnp.int32)

    n_sel = seq // SEL_BLOCK
    top_k = min(SEL_TOPK, n_sel)
    cs = np.arange(n_cmp) * CMP_STRIDE
    ss = np.arange(n_sel) * SEL_BLOCK
    cover = np.clip(np.minimum(cs[:, None] + CMP_BLOCK, ss[None, :] + SEL_BLOCK) - np.maximum(cs[:, None], ss[None, :]), 0, None) / CMP_STRIDE
    cover = jnp.asarray(cover, f32)
    k_sel = k_s.reshape(bsz, n_sel, SEL_BLOCK, G, Dh).transpose(0, 3, 1, 2, 4)
    v_sel = v_s.reshape(bsz, n_sel, SEL_BLOCK, G, Dh).transpose(0, 3, 1, 2, 4)
    pad = jnp.zeros((bsz, WINDOW, G, Dh), h.dtype)
    k_win = jnp.concatenate([pad, k_w.reshape(bsz, seq, G, Dh)], axis=1)
    v_win = jnp.concatenate([pad, v_w.reshape(bsz, seq, G, Dh)], axis=1)
    bi = jnp.arange(bsz)[:, None, None, None]
    gi = jnp.arange(G)[None, None, :, None]
    blk_id = jnp.arange(n_sel)
    n_qb = seq // Q_BLOCK
    q_blocks = jnp.moveaxis(q.reshape(bsz, n_qb, Q_BLOCK, G, J, Dh), 1, 0)
    g_blocks = jnp.moveaxis(gates.reshape(bsz, n_qb, Q_BLOCK, 3, G, J), 1, 0)

    def attend_block(args):
        qb, q_blk, g_blk = args
        start = qb * Q_BLOCK
        t = start + jnp.arange(Q_BLOCK)
        dist_c = t[:, None] - cmp_end[None, :]
        s_c = jnp.einsum('bqgjd,bngd->bqgjn', q_blk, k_cmp).astype(f32) - slopes[:, :, None] * dist_c[None, :, None, None, :].astype(f32)
        p_c = masked_softmax(s_c, (dist_c >= 0)[None, :, None, None, :])
        o_c = jnp.einsum('bqgjn,bngd->bqgjd', p_c, v_cmp)
        imp = jnp.einsum('bqgjn,nm->bqgm', p_c, cover)
        cur = (t // SEL_BLOCK)[:, None]
        forced = (blk_id[None, :] == 0) | (blk_id[None, :] == cur) | (blk_id[None, :] == cur - 1)
        causal_blk = blk_id[None, :] * SEL_BLOCK <= t[:, None]
        imp = jnp.where(forced[None, :, None, :], SEL_FORCE, jnp.where(causal_blk[None, :, None, :], imp, -SEL_FORCE))
        _, top = lax.top_k(imp, top_k)
        k_g = k_sel[bi, gi, top]
        v_g = v_sel[bi, gi, top]
        dist_s = t[None, :, None, None, None] - (top[..., None] * SEL_BLOCK + jnp.arange(SEL_BLOCK))
        s_s = jnp.einsum('bqgjd,bqgkld->bqgjkl', q_blk, k_g).astype(f32) - slopes[:, :, None, None] * dist_s[:, :, :, None].astype(f32)
        p_s = masked_softmax(s_s, (dist_s >= 0)[:, :, :, None], axis=(-2, -1))
        o_s = jnp.einsum('bqgjkl,bqgkld->bqgjd', p_s, v_g.astype(f32))
        k_blk = lax.dynamic_slice_in_dim(k_win, start, WINDOW + Q_BLOCK, axis=1)
        v_blk = lax.dynamic_slice_in_dim(v_win, start, WINDOW + Q_BLOCK, axis=1)
        pos_w = start - WINDOW + jnp.arange(WINDOW + Q_BLOCK)
        dist_w = t[:, None] - pos_w[None, :]
        valid_w = (dist_w >= 0) & (dist_w < WINDOW) & (pos_w[None, :] >= 0)
        s_w = jnp.einsum('bqgjd,bsgd->bqgjs', q_blk, k_blk).astype(f32) - slopes[:, :, None] * dist_w[None, :, None, None, :].astype(f32)
        p_w = masked_softmax(s_w, valid_w[None, :, None, None, :])
        o_w = jnp.einsum('bqgjs,bsgd->bqgjd', p_w, v_blk.astype(f32))
        o = g_blk[:, :, 0, :, :, None] * o_c + g_blk[:, :, 1, :, :, None] * o_s + g_blk[:, :, 2, :, :, None] * o_w
        return o.astype(h.dtype)

    o = lax.map(attend_block, (jnp.arange(n_qb), q_blocks, g_blocks))
    o = jnp.moveaxis(o, 0, 1).reshape(bsz, seq, NSA_OUT)
    return o @ w_out


def conv_ffn(h, w_up, conv_w, conv_b, w_down):
    u = causal_dwconv(h @ w_up, conv_w, conv_b)
    gate, up = jnp.split(u, 2, axis=-1)
    return (jax.nn.silu(gate) * up) @ w_down


def setup_inputs(seed: int = 0) -> dict:
    key = jax.random.key(seed)
    keys = iter(jax.random.split(key, 48))
    f32 = jnp.float32
    n_even = (DEPTH + 1) // 2
    n_odd = DEPTH // 2

    def normal(shape, scale):
        return jax.random.normal(next(keys), shape, f32) * scale

    def gain(shape):
        return 1.0 + normal(shape, 0.05)

    def a_log(shape):
        return jnp.log(jax.random.uniform(next(keys), shape, f32, 1.0, 16.0))

    def dt_bias(shape):
        dt = jnp.exp(jax.random.uniform(next(keys), shape, f32, math.log(1e-3), math.log(1e-1)))
        return dt + jnp.log(-jnp.expm1(-dt))

    cmp_in = CMP_BLOCK * NSA_HEAD_DIM
    return {
        'x': normal((BATCH, SEQ, D_MODEL), 1.0),
        'norm_mix': gain((DEPTH, D_MODEL)),
        'norm_ffn': gain((DEPTH, D_MODEL)),
        'norm_final': gain((D_MODEL,)),
        'hy_w_in': normal((n_even, D_MODEL, HY_IN), D_MODEL ** -0.5),
        'gdn_conv': normal((n_even, SHORT_CONV, 3 * GDN_WIDTH), SHORT_CONV ** -0.5),
        'gdn_a_log': a_log((n_even, GDN_HEADS)),
        'gdn_dt_bias': dt_bias((n_even, GDN_HEADS)),
        'gdn_norm': gain((n_even, GDN_HEAD_DIM)),
        'ssd_conv': normal((n_even, SHORT_CONV, SSD_XBC), SHORT_CONV ** -0.5),
        'ssd_conv_bias': normal((n_even, SSD_XBC), 0.02),
        'ssd_a_log': a_log((n_even, SSD_HEADS)),
        'ssd_dt_bias': dt_bias((n_even, SSD_HEADS)),
        'ssd_d': gain((n_even, SSD_HEADS)),
        'ssd_norm': gain((n_even, SSD_WIDTH)),
        'hy_w_out': normal((n_even, HY_OUT, D_MODEL), HY_OUT ** -0.5),
        'nsa_w_in': normal((n_odd, D_MODEL, NSA_IN), D_MODEL ** -0.5),
        'cmp_pos_k': normal((n_odd, CMP_BLOCK, NSA_HEAD_DIM), 0.1),
        'cmp_w1_k': normal((n_odd, cmp_in, CMP_HIDDEN), cmp_in ** -0.5),
        'cmp_w2_k': normal((n_odd, CMP_HIDDEN, NSA_HEAD_DIM), CMP_HIDDEN ** -0.5),
        'cmp_pos_v': normal((n_odd, CMP_BLOCK, NSA_HEAD_DIM), 0.1),
        'cmp_w1_v': normal((n_odd, cmp_in, CMP_HIDDEN), cmp_in ** -0.5),
        'cmp_w2_v': normal((n_odd, CMP_HIDDEN, NSA_HEAD_DIM), CMP_HIDDEN ** -0.5),
        'nsa_w_out': normal((n_odd, NSA_OUT, D_MODEL), NSA_OUT ** -0.5),
        'ffn_w_up': normal((DEPTH, D_MODEL, 2 * D_FF), D_MODEL ** -0.5),
        'ffn_conv': normal((DEPTH, FFN_CONV, 2 * D_FF), FFN_CONV ** -0.5),
        'ffn_conv_bias': normal((DEPTH, 2 * D_FF), 0.02),
        'ffn_w_down': normal((DEPTH, D_FF, D_MODEL), D_FF ** -0.5),
    }


def reference(x, norm_mix, norm_ffn, norm_final, hy_w_in, gdn_conv, gdn_a_log, gdn_dt_bias, gdn_norm, ssd_conv, ssd_conv_bias, ssd_a_log, ssd_dt_bias, ssd_d, ssd_norm, hy_w_out, nsa_w_in, cmp_pos_k, cmp_w1_k, cmp_w2_k, cmp_pos_v, cmp_w1_v, cmp_w2_v, nsa_w_out, ffn_w_up, ffn_conv, ffn_conv_bias, ffn_w_down):
    h = x
    for layer in range(DEPTH):
        e = layer // 2
        y = rms_norm(h, norm_mix[layer])
        if layer % 2 == 0:
            y = hybrid_mixer(y, hy_w_in[e], gdn_conv[e], gdn_a_log[e], gdn_dt_bias[e], gdn_norm[e], ssd_conv[e], ssd_conv_bias[e], ssd_a_log[e], ssd_dt_bias[e], ssd_d[e], ssd_norm[e], hy_w_out[e])
        else:
            y = nsa_mixer(y, nsa_w_in[e], cmp_pos_k[e], cmp_w1_k[e], cmp_w2_k[e], cmp_pos_v[e], cmp_w1_v[e], cmp_w2_v[e], nsa_w_out[e])
        h = h + y
        h = h + conv_ffn(rms_norm(h, norm_ffn[layer]), ffn_w_up[layer], ffn_conv[layer], ffn_conv_bias[layer], ffn_w_down[layer])
    return rms_norm(h, norm_final)
```

```python
import functools
import math

import numpy as np
import jax
import jax.numpy as jnp
from jax import lax
from jax.experimental import pallas as pl
from jax.experimental.pallas import tpu as pltpu

F32 = jnp.float32
BF16 = jnp.bfloat16
HIGHEST = lax.Precision.HIGHEST

EPS = 1e-6
LANES = 128
VMEM_LIMIT = 56 * 1024 * 1024

GDN_HEADS = 4
GDN_HEAD_DIM = 128
GDN_WIDTH = GDN_HEADS * GDN_HEAD_DIM
CHUNK = 64
SHORT_CONV = 4
SSD_HEADS = 8
SSD_HEAD_DIM = 64
SSD_WIDTH = SSD_HEADS * SSD_HEAD_DIM
SSD_GROUPS = 2
SSD_STATE = 128
SSD_XBC = SSD_WIDTH + 2 * SSD_GROUPS * SSD_STATE
NSA_HEADS = 16
NSA_KV_GROUPS = 2
NSA_HEAD_DIM = 64
NSA_KV = NSA_KV_GROUPS * NSA_HEAD_DIM
NSA_J = NSA_HEADS // NSA_KV_GROUPS
CMP_BLOCK = 32
CMP_STRIDE = 16
CMP_HIDDEN = 256
SEL_BLOCK = 64
SEL_TOPK = 6
WINDOW = 512
SEL_FORCE = 1e9
NEG_BIG = -1e30
FFN_CONV = 3
FF_CHUNK = 256


def _cparams(*sem):
    return pltpu.CompilerParams(dimension_semantics=sem, vmem_limit_bytes=VMEM_LIMIT)


def _const_spec(shape):
    nd = len(shape)
    return pl.BlockSpec(shape, lambda *_: (0,) * nd, pipeline_mode=pl.Buffered(1))


def _rms(x, w):
    return x * lax.rsqrt(jnp.mean(x * x, axis=-1, keepdims=True) + EPS) * w


def _silu(x):
    return x * (1.0 / (1.0 + jnp.exp(-x)))


def _softplus(x):
    return jnp.maximum(x, 0.0) + jnp.log1p(jnp.exp(-jnp.abs(x)))


def _bdot(a, b):
    return jnp.dot(a.astype(BF16), b.astype(BF16), preferred_element_type=F32)


def _bdot_nt(a, b):
    return lax.dot_general(a.astype(BF16), b.astype(BF16), (((1,), (1,)), ((), ())), preferred_element_type=F32)


def _bdot_tn(a, b):
    return lax.dot_general(a.astype(BF16), b.astype(BF16), (((0,), (0,)), ((), ())), preferred_element_type=F32)


def _fdot(a, b):
    return jnp.dot(a, b, preferred_element_type=F32, precision=HIGHEST)


def _norm_proj_kernel(x_ref, nw_ref, *refs, n_out):
    xn = _rms(x_ref[...], nw_ref[...]).astype(BF16)
    for w_ref, o_ref in zip(refs[:n_out], refs[n_out:]):
        o_ref[...] = jnp.dot(xn, w_ref[...], preferred_element_type=F32).astype(o_ref.dtype)


def norm_proj(h2, nw, weights, tm=512):
    T, D = h2.shape
    n_out = len(weights)
    return pl.pallas_call(
        functools.partial(_norm_proj_kernel, n_out=n_out),
        grid=(T // tm,),
        in_specs=[pl.BlockSpec((tm, D), lambda i: (i, 0)), _const_spec((1, D))]
        + [_const_spec(w.shape) for w in weights],
        out_specs=[pl.BlockSpec((tm, w.shape[1]), lambda i: (i, 0)) for w in weights],
        out_shape=[jax.ShapeDtypeStruct((T, w.shape[1]), F32) for w in weights],
        compiler_params=_cparams("parallel"),
        name="norm_proj",
    )(h2, nw.reshape(1, D), *weights)


def _proj_residual_kernel(h_ref, *refs, n_in):
    acc = h_ref[...]
    for a_ref, w_ref in zip(refs[:n_in], refs[n_in:2 * n_in]):
        acc = acc + jnp.dot(a_ref[...].astype(BF16), w_ref[...], preferred_element_type=F32)
    refs[2 * n_in][...] = acc


def proj_residual(h2, acts, weights, tm=512):
    T, D = h2.shape
    n_in = len(acts)
    return pl.pallas_call(
        functools.partial(_proj_residual_kernel, n_in=n_in),
        grid=(T // tm,),
        in_specs=[pl.BlockSpec((tm, D), lambda i: (i, 0))]
        + [pl.BlockSpec((tm, a.shape[1]), lambda i: (i, 0)) for a in acts]
        + [_const_spec(w.shape) for w in weights],
        out_specs=pl.BlockSpec((tm, D), lambda i: (i, 0)),
        out_shape=jax.ShapeDtypeStruct((T, D), F32),
        compiler_params=_cparams("parallel"),
        name="proj_residual",
    )(h2, *acts, *weights)


def _ffn_kernel(h_ref, nw_ref, wg_ref, wu_ref, cwg_ref, cwu_ref, cbg_ref, cbu_ref, wd_ref, fw_ref, o_ref,
                xn_ref, acc_ref, pg_ref, pu_ref, *, n_chunks, final_norm):
    tm = h_ref.shape[0]

    @pl.when(pl.program_id(1) == 0)
    def _():
        pg_ref[...] = jnp.zeros_like(pg_ref)
        pu_ref[...] = jnp.zeros_like(pu_ref)

    x = h_ref[...]
    xn_ref[...] = _rms(x, nw_ref[...]).astype(BF16)
    acc_ref[...] = x
    row = lax.broadcasted_iota(jnp.int32, (tm, 1), 0)

    def conv(u, prev, w, b):
        m1 = jnp.where(row == 0, prev[7:8], pltpu.roll(u, 1, 0))
        m2 = jnp.where(row == 0, prev[6:7], jnp.where(row == 1, prev[7:8], pltpu.roll(u, 2, 0)))
        return u * w[2:3] + m1 * w[1:2] + m2 * w[0:1] + b

    def body(c, carry):
        xn = xn_ref[...]
        ug = jnp.dot(xn, wg_ref[c], preferred_element_type=F32)
        uu = jnp.dot(xn, wu_ref[c], preferred_element_type=F32)
        yg = conv(ug, pg_ref[c], cwg_ref[c], cbg_ref[c])
        yu = conv(uu, pu_ref[c], cwu_ref[c], cbu_ref[c])
        pg_ref[c] = ug[tm - 8:tm]
        pu_ref[c] = uu[tm - 8:tm]
        a = (_silu(yg) * yu).astype(BF16)
        acc_ref[...] += jnp.dot(a, wd_ref[c], preferred_element_type=F32)
        return carry

    lax.fori_loop(0, n_chunks, body, 0)
    out = acc_ref[...]
    if final_norm:
        out = _rms(out, fw_ref[...])
    o_ref[...] = out


def conv_ffn(h3, nw, w_up, conv_w, conv_b, w_down, final_w, final_norm, tm=512):
    B, S, D = h3.shape
    dff = w_down.shape[0]
    nch = dff // FF_CHUNK

    def chunked_cols(m):
        return m.reshape(m.shape[0], nch, FF_CHUNK).transpose(1, 0, 2)

    wg = chunked_cols(w_up[:, :dff]).astype(BF16)
    wu = chunked_cols(w_up[:, dff:]).astype(BF16)
    cwg = chunked_cols(conv_w[:, :dff])
    cwu = chunked_cols(conv_w[:, dff:])
    cbg = chunked_cols(conv_b[None, :dff])
    cbu = chunked_cols(conv_b[None, dff:])
    wd = w_down.reshape(nch, FF_CHUNK, D).astype(BF16)
    tile = pl.BlockSpec((None, tm, D), lambda b, s: (b, s, 0))
    return pl.pallas_call(
        functools.partial(_ffn_kernel, n_chunks=nch, final_norm=final_norm),
        grid=(B, S // tm),
        in_specs=[tile, _const_spec((1, D))] + [_const_spec(a.shape) for a in (wg, wu, cwg, cwu, cbg, cbu, wd)]
        + [_const_spec((1, D))],
        out_specs=tile,
        out_shape=jax.ShapeDtypeStruct((B, S, D), F32),
        scratch_shapes=[pltpu.VMEM((tm, D), BF16), pltpu.VMEM((tm, D), F32),
                        pltpu.VMEM((nch, 8, FF_CHUNK), F32), pltpu.VMEM((nch, 8, FF_CHUNK), F32)],
        compiler_params=_cparams("parallel", "arbitrary"),
        name="conv_ffn",
    )(h3, nw.reshape(1, D), wg, wu, cwg, cwu, cbg, cbu, wd, final_w.reshape(1, D))


def _causal_conv(x, xbuf, w, ts, width):
    xbuf[8:8 + ts, :] = x
    y = x * w[width - 1:width]
    for k in range(1, width):
        y = y + xbuf[pl.ds(8 - k, ts), :] * w[width - 1 - k:width - k]
    xbuf[0:8, :] = x[ts - 8:ts]
    return y


def _gdn_kernel(qkv_ref, sm_ref, z_ref, cw_ref, alog_ref, dtb_ref, gn_ref, o_ref,
                xbuf, qn, kn, vn, gb, state, *, ts):
    nh, dk, C = GDN_HEADS, GDN_HEAD_DIM, CHUNK
    R = nh * C

    @pl.when(pl.program_id(1) == 0)
    def _():
        xbuf[0:8, :] = jnp.zeros((8, xbuf.shape[1]), F32)
        state[...] = jnp.zeros_like(state)

    y = _silu(_causal_conv(qkv_ref[...], xbuf, cw_ref[...], ts, SHORT_CONV))
    for h in range(nh):
        qh = y[:, h * dk:(h + 1) * dk]
        qn[h] = qh * (lax.rsqrt(jnp.sum(qh * qh, axis=-1, keepdims=True) + EPS) * dk ** -0.5)
        kh = y[:, GDN_WIDTH + h * dk:GDN_WIDTH + (h + 1) * dk]
        kn[h] = kh * lax.rsqrt(jnp.sum(kh * kh, axis=-1, keepdims=True) + EPS)
        vn[h] = y[:, 2 * GDN_WIDTH + h * dk:2 * GDN_WIDTH + (h + 1) * dk]
    sm = sm_ref[...]
    lane = lax.broadcasted_iota(jnp.int32, sm.shape, 1)
    beta = 1.0 / (1.0 + jnp.exp(-sm))
    g = -jnp.exp(alog_ref[...]) * _softplus(sm + dtb_ref[...])
    gb[...] = jnp.where(lane < nh, beta, g)

    ri = lax.broadcasted_iota(jnp.int32, (R, R), 0)
    ci = lax.broadcasted_iota(jnp.int32, (R, R), 1)
    same_head = (ri >> 6) == (ci >> 6)
    causal = same_head & (ri >= ci)
    strict = same_head & (ri > ci)
    eye = (ri == ci).astype(F32)
    l64 = (lax.broadcasted_iota(jnp.int32, (C, C), 0) >= lax.broadcasted_iota(jnp.int32, (C, C), 1)).astype(F32)
    gnw = gn_ref[...]

    def stack(ref, r0):
        return jnp.concatenate([ref[h, pl.ds(r0, C), :] for h in range(nh)], axis=0)

    def chunk(c, carry):
        r0 = pl.multiple_of(c * C, C)
        gbc = gb[pl.ds(r0, C), :]
        gam = _fdot(l64, gbc)
        q, k, v = stack(qn, r0), stack(kn, r0), stack(vn, r0)
        beta_col = jnp.concatenate([gbc[:, h:h + 1] for h in range(nh)], axis=0)
        gam_col = jnp.concatenate([gam[:, nh + h:nh + h + 1] for h in range(nh)], axis=0)
        gam_row = jnp.sum(eye * gam_col, axis=0, keepdims=True)
        decay = jnp.exp(jnp.where(causal, gam_col - gam_row, NEG_BIG))
        kb = k * beta_col
        p = jnp.where(strict, -(_bdot_nt(kb, k) * decay), 0.0)
        t = eye + p
        for _ in range(5):
            p = _bdot(p, p)
            t = t + _bdot(t, p)
        eg = jnp.exp(gam_col)
        sol = _bdot(t, jnp.concatenate([v * beta_col, kb * eg], axis=1))
        qk = _bdot_nt(q, k) * decay
        q_dec = q * eg
        gl = [gam[C - 1:C, nh + h:nh + h + 1] for h in range(nh)]
        gl_col = jnp.concatenate([jnp.broadcast_to(gl[h], (C, 1)) for h in range(nh)], axis=0)
        k_dec = k * jnp.exp(gl_col - gam_col)
        for h in range(nh):
            rows = slice(h * C, (h + 1) * C)
            s_h = state[h]
            u = sol[rows, :dk] - _bdot(sol[rows, dk:], s_h)
            o = _bdot(q_dec[rows], s_h) + _bdot(qk[rows, h * C:(h + 1) * C], u)
            state[h] = s_h * jnp.exp(gl[h]) + _bdot_tn(k_dec[rows], u)
            zh = z_ref[pl.ds(r0, C), h * dk:(h + 1) * dk]
            o_ref[pl.ds(r0, C), h * dk:(h + 1) * dk] = _rms(o, gnw) * _silu(zh)
        return carry

    lax.fori_loop(0, ts // C, chunk, 0)


def gdn_heads(qkv, small, z_a, conv_w, a_log, dt_bias, gnorm, ts=256):
    B, S, _ = qkv.shape
    nh, dk = GDN_HEADS, GDN_HEAD_DIM
    alog_l = jnp.zeros((1, LANES), F32).at[0, nh:2 * nh].set(a_log)
    dtb_l = jnp.zeros((1, LANES), F32).at[0, nh:2 * nh].set(dt_bias)

    def tile(n):
        return pl.BlockSpec((None, ts, n), lambda b, s: (b, s, 0))

    return pl.pallas_call(
        functools.partial(_gdn_kernel, ts=ts),
        grid=(B, S // ts),
        in_specs=[tile(3 * GDN_WIDTH), tile(LANES), tile(GDN_WIDTH), _const_spec(conv_w.shape),
                  _const_spec((1, LANES)), _const_spec((1, LANES)), _const_spec((1, dk))],
        out_specs=tile(GDN_WIDTH),
        out_shape=jax.ShapeDtypeStruct((B, S, GDN_WIDTH), F32),
        scratch_shapes=[pltpu.VMEM((ts + 8, 3 * GDN_WIDTH), F32)]
        + [pltpu.VMEM((nh, ts, dk), F32)] * 3
        + [pltpu.VMEM((ts, LANES), F32), pltpu.VMEM((nh, dk, dk), F32)],
        compiler_params=_cparams("parallel", "arbitrary"),
        name="gdn_heads",
    )(qkv, small, z_a, conv_w, alog_l, dtb_l, gnorm.reshape(1, dk))


def _ssd_kernel(xbc_ref, sm_ref, z_ref, cw_ref, cb_ref, dtb_ref, alog_ref, d_ref, nw_ref, e_ref, o_ref,
                xbuf, xs_s, b_s, c_s, dt_s, hstate, *, ts):
    C, P, N = CHUNK, SSD_HEAD_DIM, SSD_STATE
    hpg = SSD_HEADS // SSD_GROUPS
    gw = hpg * P

    @pl.when(pl.program_id(1) == 0)
    def _():
        xbuf[0:8, :] = jnp.zeros((8, xbuf.shape[1]), F32)
        hstate[...] = jnp.zeros_like(hstate)

    y = _silu(_causal_conv(xbc_ref[...], xbuf, cw_ref[...], ts, SHORT_CONV) + cb_ref[...])
    xs_s[...] = y[:, :SSD_WIDTH]
    b_s[...] = y[:, SSD_WIDTH:SSD_WIDTH + SSD_GROUPS * N]
    c_s[...] = y[:, SSD_WIDTH + SSD_GROUPS * N:]
    sm = sm_ref[...]
    lane = lax.broadcasted_iota(jnp.int32, sm.shape, 1)
    dt = jnp.where((lane >= SSD_HEADS) & (lane < 2 * SSD_HEADS), _softplus(sm + dtb_ref[...]), 0.0)
    dt_s[...] = _fdot(dt, e_ref[...])
    a_l = -jnp.exp(alog_ref[...])
    d_l = d_ref[...]
    nw = nw_ref[...]

    li = lax.broadcasted_iota(jnp.int32, (C, C), 0)
    si = lax.broadcasted_iota(jnp.int32, (C, C), 1)
    causal = li >= si
    l64 = causal.astype(F32)
    eye = (li == si).astype(F32)

    def chunk(c, carry):
        r0 = pl.multiple_of(c * C, C)
        dtc = dt_s[pl.ds(r0, C), :]
        xs = xs_s[pl.ds(r0, C), :]
        xc = xs * dtc
        acs = _fdot(l64, dtc * a_l)
        acs_last = acs[C - 1:C, :]
        eacs = jnp.exp(acs)
        xd = xc * jnp.exp(acs_last - acs)
        e_last = jnp.exp(acs_last)
        ys = []
        for g in range(SSD_GROUPS):
            gl = slice(g * gw, (g + 1) * gw)
            bg = b_s[pl.ds(r0, C), g * N:(g + 1) * N]
            cg = c_s[pl.ds(r0, C), g * N:(g + 1) * N]
            cb = _bdot_nt(cg, bg)
            h_in = hstate[g]
            y_off = _bdot(cg, h_in) * eacs[:, gl]
            yd = []
            for j in range(hpg):
                hl = slice((g * hpg + j) * P, (g * hpg + j + 1) * P)
                blk = acs[:, hl]
                row = jnp.sum(blk * eye, axis=0, keepdims=True)
                m = cb * jnp.exp(jnp.where(causal, blk - row, NEG_BIG))
                yd.append(_bdot(m, xc[:, hl]))
            hstate[g] = h_in * e_last[:, gl] + _bdot_tn(bg, xd[:, gl])
            ys.append(jnp.concatenate(yd, axis=1) + y_off)
        yy = jnp.concatenate(ys, axis=1) + d_l * xs
        yy = yy * _silu(z_ref[pl.ds(r0, C), :])
        o_ref[pl.ds(r0, C), :] = jnp.concatenate(
            [_rms(yy[:, g * gw:(g + 1) * gw], nw[:, g * gw:(g + 1) * gw]) for g in range(SSD_GROUPS)], axis=1)
        return carry

    lax.fori_loop(0, ts // C, chunk, 0)


def ssd_heads(xbc, small, z_b, conv_w, conv_b, a_log, dt_bias, d_skip, norm_w, ts=256):
    B, S, _ = xbc.shape
    nh, P = SSD_HEADS, SSD_HEAD_DIM
    dtb_l = jnp.zeros((1, LANES), F32).at[0, nh:2 * nh].set(dt_bias)
    expand = np.zeros((LANES, SSD_WIDTH), np.float32)
    for h in range(nh):
        expand[nh + h, h * P:(h + 1) * P] = 1.0

    def tile(n):
        return pl.BlockSpec((None, ts, n), lambda b, s: (b, s, 0))

    gw = SSD_WIDTH // SSD_GROUPS
    return pl.pallas_call(
        functools.partial(_ssd_kernel, ts=ts),
        grid=(B, S // ts),
        in_specs=[tile(SSD_XBC), tile(LANES), tile(SSD_WIDTH), _const_spec(conv_w.shape), _const_spec((1, SSD_XBC)),
                  _const_spec((1, LANES)), _const_spec((1, SSD_WIDTH)), _const_spec((1, SSD_WIDTH)),
                  _const_spec((1, SSD_WIDTH)), _const_spec((LANES, SSD_WIDTH))],
        out_specs=tile(SSD_WIDTH),
        out_shape=jax.ShapeDtypeStruct((B, S, SSD_WIDTH), F32),
        scratch_shapes=[pltpu.VMEM((ts + 8, SSD_XBC), F32), pltpu.VMEM((ts, SSD_WIDTH), F32),
                        pltpu.VMEM((ts, SSD_GROUPS * SSD_STATE), F32), pltpu.VMEM((ts, SSD_GROUPS * SSD_STATE), F32),
                        pltpu.VMEM((ts, SSD_WIDTH), F32), pltpu.VMEM((SSD_GROUPS, SSD_STATE, gw), F32)],
        compiler_params=_cparams("parallel", "arbitrary"),
        name="ssd_heads",
    )(xbc, small, z_b, conv_w, conv_b.reshape(1, SSD_XBC), dtb_l, jnp.repeat(a_log, P).reshape(1, SSD_WIDTH),
      jnp.repeat(d_skip, P).reshape(1, SSD_WIDTH), norm_w.reshape(1, SSD_WIDTH), jnp.asarray(expand))


def hybrid_layer(h, nw, w_in, gdn_conv, gdn_a_log, gdn_dt_bias, gdn_norm, ssd_conv, ssd_conv_bias, ssd_a_log,
                 ssd_dt_bias, ssd_d, ssd_norm, w_out):
    B, S, D = h.shape
    T = B * S
    gwid, swid = GDN_WIDTH, SSD_WIDTH
    o_za = 3 * gwid
    o_ba = o_za + gwid
    o_zb = o_ba + 2 * GDN_HEADS
    o_xbc = o_zb + swid
    o_dt = o_xbc + SSD_XBC
    w_small = jnp.zeros((D, LANES), F32)
    w_small = w_small.at[:, :2 * GDN_HEADS].set(w_in[:, o_ba:o_zb])
    w_small = w_small.at[:, SSD_HEADS:2 * SSD_HEADS].set(w_in[:, o_dt:o_dt + SSD_HEADS])
    ws = [w_in[:, :o_za], w_in[:, o_za:o_ba], w_small, w_in[:, o_zb:o_xbc], w_in[:, o_xbc:o_dt]]
    h2 = h.reshape(T, D)
    qkv, z_a, small, z_b, xbc = norm_proj(h2, nw, [w.astype(BF16) for w in ws])
    small3 = small.reshape(B, S, LANES)
    o_a = gdn_heads(qkv.reshape(B, S, -1), small3, z_a.reshape(B, S, -1), gdn_conv, gdn_a_log, gdn_dt_bias, gdn_norm)
    y_b = ssd_heads(xbc.reshape(B, S, -1), small3, z_b.reshape(B, S, -1), ssd_conv, ssd_conv_bias, ssd_a_log,
                    ssd_dt_bias, ssd_d, ssd_norm)
    out = proj_residual(h2, [o_a.reshape(T, gwid), y_b.reshape(T, swid)],
                        [w_out[:gwid].astype(BF16), w_out[gwid:].astype(BF16)])
    return out.reshape(B, S, D)


def _compress_kernel(kc_ref, vc_ref, pk0_ref, pk1_ref, pv0_ref, pv1_ref, w1ka_ref, w1kb_ref, w1va_ref, w1vb_ref,
                     w2k_ref, w2v_ref, ko_ref, vo_ref):
    n_seg = kc_ref.shape[1]
    for src, p0, p1, wa, wb, w2, out in ((kc_ref, pk0_ref, pk1_ref, w1ka_ref, w1kb_ref, w2k_ref, ko_ref),
                                         (vc_ref, pv0_ref, pv1_ref, w1va_ref, w1vb_ref, w2v_ref, vo_ref)):
        for g in range(NSA_KV_GROUPS):
            t = src[g]
            f0 = _bdot(t + p0[...], wa[...])
            f1 = _bdot(t + p1[...], wb[...])
            hid = f0 + pltpu.roll(f1, n_seg - 1, 0)
            out[g] = _bdot(_silu(hid), w2[...])


def nsa_compress(kc_r, vc_r, pos_k, w1_k, w2_k, pos_v, w1_v, w2_v):
    B, G, n_seg, seg_w = kc_r.shape
    half = CMP_STRIDE
    args = [kc_r, vc_r,
            pos_k[:half].reshape(1, seg_w), pos_k[half:].reshape(1, seg_w),
            pos_v[:half].reshape(1, seg_w), pos_v[half:].reshape(1, seg_w),
            w1_k[:seg_w].astype(BF16), w1_k[seg_w:].astype(BF16), w1_v[:seg_w].astype(BF16), w1_v[seg_w:].astype(BF16),
            w2_k.astype(BF16), w2_v.astype(BF16)]
    per_b = pl.BlockSpec((None, G, n_seg, seg_w), lambda b: (b, 0, 0, 0))
    out_b = pl.BlockSpec((None, G, n_seg, NSA_HEAD_DIM), lambda b: (b, 0, 0, 0))
    return pl.pallas_call(
        _compress_kernel,
        grid=(B,),
        in_specs=[per_b, per_b] + [_const_spec(a.shape) for a in args[2:]],
        out_specs=[out_b, out_b],
        out_shape=[jax.ShapeDtypeStruct((B, G, n_seg, NSA_HEAD_DIM), F32)] * 2,
        compiler_params=_cparams("parallel"),
        name="nsa_compress",
    )(*args)


def _alibi_slopes():
    return [[float(2.0 ** (-8.0 * (g * NSA_J + j + 1) / NSA_HEADS)) for j in range(NSA_J)]
            for g in range(NSA_KV_GROUPS)]


M_INIT = -1e20


def _softmax_step(carry, s, pen, v):
    m, l, acc = carry
    s = s - pen
    m_new = jnp.maximum(m, jnp.max(s, axis=-1, keepdims=True))
    alpha = jnp.exp(m - m_new)
    e = jnp.exp(s - m_new)
    l = alpha * l + jnp.sum(e, axis=-1, keepdims=True)
    acc = alpha * acc + _bdot(e, v)
    return m_new, l, acc


def _nsa_kernel(q_ref, gt_ref, kc_ref, vc_ref, ks_ref, vs_ref, kw_ref, vw_ref, cov_ref, o_ref,
                *, tq, tk, seq, n_cmp, n_sel):
    G, J, Dh = NSA_KV_GROUPS, NSA_J, NSA_HEAD_DIM
    slopes = _alibi_slopes()
    start = pl.program_id(1) * tq
    t_col = start + lax.broadcasted_iota(jnp.int32, (tq, 1), 0)
    gates = 1.0 / (1.0 + jnp.exp(-gt_ref[...]))
    nc = kc_ref.shape[1]
    rows = J * tq
    top_k = min(SEL_TOPK, n_sel)

    def init():
        return (jnp.full((rows, 1), M_INIT, F32), jnp.zeros((rows, 1), F32), jnp.zeros((rows, Dh), F32))

    def penalty(dist, valid, g):
        distf = dist.astype(F32)
        mask_pen = jnp.where(valid, 0.0, -NEG_BIG)
        return jnp.concatenate([slopes[g][j] * distf + mask_pen for j in range(J)], axis=0)

    def finish(carry):
        _, l, acc = carry
        return acc / jnp.maximum(l, 1e-30)

    for g in range(G):
        qg = jnp.concatenate([q_ref[:, (g * J + j) * Dh:(g * J + j + 1) * Dh] for j in range(J)], axis=0) * Dh ** -0.5

        n_idx = lax.broadcasted_iota(jnp.int32, (tq, nc), 1)
        dist_c = t_col - (n_idx * CMP_STRIDE + (CMP_BLOCK - 1))
        valid_c = (dist_c >= 0) & (n_idx < n_cmp)
        s_c = lax.dot_general(qg, kc_ref[g], (((1,), (1,)), ((), ())), preferred_element_type=F32, precision=HIGHEST)
        s_c = s_c - penalty(dist_c, valid_c, g)
        m_c = jnp.maximum(jnp.max(s_c, axis=-1, keepdims=True), M_INIT)
        e_c = jnp.exp(s_c - m_c)
        p_c = e_c / jnp.maximum(jnp.sum(e_c, axis=-1, keepdims=True), 1e-30)
        o_c = _bdot(p_c, vc_ref[g])
        p_sum = p_c[0:tq]
        for j in range(1, J):
            p_sum = p_sum + p_c[j * tq:(j + 1) * tq]
        imp = _fdot(p_sum, cov_ref[...])

        blk = lax.broadcasted_iota(jnp.int32, (tq, LANES), 1)
        cur = t_col >> 6
        forced = (blk == 0) | (blk == cur) | (blk == cur - 1)
        imp = jnp.where(forced, SEL_FORCE, jnp.where(blk * SEL_BLOCK <= t_col, imp, -SEL_FORCE))
        imp = jnp.where(blk < n_sel, imp, -3e38)
        sel = jnp.zeros((tq, LANES), F32)
        for _ in range(top_k):
            mx = jnp.max(imp, axis=-1, keepdims=True)
            first = jnp.min(jnp.where(imp == mx, blk, LANES), axis=-1, keepdims=True)
            hit = blk == first
            sel = jnp.where(hit, 1.0, sel)
            imp = jnp.where(hit, -jnp.inf, imp)

        def sel_step(kt, carry):
            k0 = pl.multiple_of(kt * tk, tk)
            pos = k0 + lax.broadcasted_iota(jnp.int32, (tq, tk), 1)
            e_blk = lax.broadcasted_iota(jnp.int32, (LANES, tk), 0)
            e_pos = k0 + lax.broadcasted_iota(jnp.int32, (LANES, tk), 1)
            expand = ((e_pos >> 6) == e_blk).astype(BF16)
            chosen = jnp.dot(sel.astype(BF16), expand, preferred_element_type=F32) > 0.5
            dist = t_col - pos
            s = _bdot_nt(qg, ks_ref[pl.ds(k0, tk), g * Dh:(g + 1) * Dh])
            return _softmax_step(carry, s, penalty(dist, chosen & (dist >= 0), g),
                                 vs_ref[pl.ds(k0, tk), g * Dh:(g + 1) * Dh])

        n_kt = (start + tq + tk - 1) // tk
        o_s = finish(lax.fori_loop(0, n_kt, sel_step, init()))

        band0 = jnp.clip(start - WINDOW, 0, seq - (WINDOW + tq))

        def win_step(kt, carry):
            k0 = pl.multiple_of(band0 + kt * tq, tq)
            pos = k0 + lax.broadcasted_iota(jnp.int32, (tq, tq), 1)
            dist = t_col - pos
            s = _bdot_nt(qg, kw_ref[pl.ds(k0, tq), g * Dh:(g + 1) * Dh])
            return _softmax_step(carry, s, penalty(dist, (dist >= 0) & (dist < WINDOW), g),
                                 vw_ref[pl.ds(k0, tq), g * Dh:(g + 1) * Dh])

        o_w = finish(lax.fori_loop(0, (WINDOW + tq) // tq, win_step, init()))

        outs = []
        for j in range(J):
            r = slice(j * tq, (j + 1) * tq)
            col = g * J + j
            outs.append(gates[:, col:col + 1] * o_c[r] + gates[:, NSA_HEADS + col:NSA_HEADS + col + 1] * o_s[r]
                        + gates[:, 2 * NSA_HEADS + col:2 * NSA_HEADS + col + 1] * o_w[r])
        o_ref[:, g * J * Dh:(g + 1) * J * Dh] = jnp.concatenate(outs, axis=1)


def nsa_attention(q, gates, k_cmp, v_cmp, k_s, v_s, k_w, v_w, tq=128, tk=256):
    B, S, _ = q.shape
    n_seg = k_cmp.shape[2]
    n_cmp = (S - CMP_BLOCK) // CMP_STRIDE + 1
    n_sel = S // SEL_BLOCK
    cs = np.arange(n_seg) * CMP_STRIDE
    ss = np.arange(n_sel) * SEL_BLOCK
    cover = np.clip(np.minimum(cs[:, None] + CMP_BLOCK, ss[None, :] + SEL_BLOCK)
                    - np.maximum(cs[:, None], ss[None, :]), 0, None) / CMP_STRIDE
    cover_pad = np.zeros((n_seg, LANES), np.float32)
    cover_pad[:n_cmp, :n_sel] = cover[:n_cmp]
    width = NSA_HEADS * NSA_HEAD_DIM

    def tile(n):
        return pl.BlockSpec((None, tq, n), lambda b, s: (b, s, 0))

    per_b_cmp = pl.BlockSpec((None, NSA_KV_GROUPS, n_seg, NSA_HEAD_DIM), lambda b, s: (b, 0, 0, 0))
    per_b_seq = pl.BlockSpec((None, S, NSA_KV), lambda b, s: (b, 0, 0))
    return pl.pallas_call(
        functools.partial(_nsa_kernel, tq=tq, tk=tk, seq=S, n_cmp=n_cmp, n_sel=n_sel),
        grid=(B, S // tq),
        in_specs=[tile(width), tile(LANES), per_b_cmp, per_b_cmp, per_b_seq, per_b_seq, per_b_seq, per_b_seq,
                  _const_spec((n_seg, LANES))],
        out_specs=tile(width),
        out_shape=jax.ShapeDtypeStruct((B, S, width), F32),
        compiler_params=_cparams("parallel", "arbitrary"),
        name="nsa_attention",
    )(q, gates, k_cmp, v_cmp, k_s, v_s, k_w, v_w, jnp.asarray(cover_pad))


def nsa_layer(h, nw, w_in, cmp_pos_k, cmp_w1_k, cmp_w2_k, cmp_pos_v, cmp_w1_v, cmp_w2_v, w_out):
    B, S, D = h.shape
    T = B * S
    width = NSA_HEADS * NSA_HEAD_DIM
    G, Dh = NSA_KV_GROUPS, NSA_HEAD_DIM
    w_gates = jnp.zeros((D, LANES), F32).at[:, :3 * NSA_HEADS].set(w_in[:, width + 6 * NSA_KV:])
    ws = [w_in[:, :width]] + [w_in[:, width + i * NSA_KV:width + (i + 1) * NSA_KV] for i in range(6)] + [w_gates]
    h2 = h.reshape(T, D)
    q, k_c, v_c, k_s, v_s, k_w, v_w, gates = norm_proj(h2, nw, [w.astype(BF16) for w in ws])
    n_seg = S // CMP_STRIDE

    def segments(t):
        return t.reshape(B, S, G, Dh).transpose(0, 2, 1, 3).reshape(B, G, n_seg, CMP_STRIDE * Dh)

    k_cmp, v_cmp = nsa_compress(segments(k_c), segments(v_c), cmp_pos_k, cmp_w1_k, cmp_w2_k,
                                cmp_pos_v, cmp_w1_v, cmp_w2_v)
    sq = lambda t: t.reshape(B, S, -1)
    o = nsa_attention(sq(q), sq(gates), k_cmp, v_cmp, sq(k_s), sq(v_s), sq(k_w), sq(v_w))
    out = proj_residual(h2, [o.reshape(T, width)], [w_out.astype(BF16)])
    return out.reshape(B, S, D)


def kernel(x, norm_mix, norm_ffn, norm_final, hy_w_in, gdn_conv, gdn_a_log, gdn_dt_bias, gdn_norm, ssd_conv, ssd_conv_bias, ssd_a_log, ssd_dt_bias, ssd_d, ssd_norm, hy_w_out, nsa_w_in, cmp_pos_k, cmp_w1_k, cmp_w2_k, cmp_pos_v, cmp_w1_v, cmp_w2_v, nsa_w_out, ffn_w_up, ffn_conv, ffn_conv_bias, ffn_w_down):
    depth = norm_mix.shape[0]
    h = x
    for layer in range(depth):
        e = layer // 2
        if layer % 2 == 0:
            h = hybrid_layer(h, norm_mix[layer], hy_w_in[e], gdn_conv[e], gdn_a_log[e], gdn_dt_bias[e], gdn_norm[e],
                             ssd_conv[e], ssd_conv_bias[e], ssd_a_log[e], ssd_dt_bias[e], ssd_d[e], ssd_norm[e],
                             hy_w_out[e])
        else:
            h = nsa_layer(h, norm_mix[layer], nsa_w_in[e], cmp_pos_k[e], cmp_w1_k[e], cmp_w2_k[e], cmp_pos_v[e],
                          cmp_w1_v[e], cmp_w2_v[e], nsa_w_out[e])
        h = conv_ffn(h, norm_ffn[layer], ffn_w_up[layer], ffn_conv[layer], ffn_conv_bias[layer], ffn_w_down[layer],
                     norm_final, layer == depth - 1)
    return h
```

```python
import functools
import math

import numpy as np
import jax
import jax.numpy as jnp
from jax import lax
from jax.experimental import pallas as pl
from jax.experimental.pallas import tpu as pltpu

F32 = jnp.float32
BF16 = jnp.bfloat16
HIGHEST = lax.Precision.HIGHEST

EPS = 1e-6
LANES = 128
VMEM_LIMIT = 56 * 1024 * 1024

GDN_HEADS = 4
GDN_HEAD_DIM = 128
GDN_WIDTH = GDN_HEADS * GDN_HEAD_DIM
CHUNK = 64
SHORT_CONV = 4
SSD_HEADS = 8
SSD_HEAD_DIM = 64
SSD_WIDTH = SSD_HEADS * SSD_HEAD_DIM
SSD_GROUPS = 2
SSD_STATE = 128
SSD_XBC = SSD_WIDTH + 2 * SSD_GROUPS * SSD_STATE
NSA_HEADS = 16
NSA_KV_GROUPS = 2
NSA_HEAD_DIM = 64
NSA_KV = NSA_KV_GROUPS * NSA_HEAD_DIM
NSA_J = NSA_HEADS // NSA_KV_GROUPS
CMP_BLOCK = 32
CMP_STRIDE = 16
CMP_HIDDEN = 256
SEL_BLOCK = 64
SEL_TOPK = 6
WINDOW = 512
SEL_FORCE = 1e9
NEG_BIG = -1e30
FFN_CONV = 3
FF_CHUNK = 256


def _cparams(*sem):
    return pltpu.CompilerParams(dimension_semantics=sem, vmem_limit_bytes=VMEM_LIMIT)


def _const_spec(shape):
    nd = len(shape)
    return pl.BlockSpec(shape, lambda *_: (0,) * nd, pipeline_mode=pl.Buffered(1))


def _rms(x, w):
    return x * lax.rsqrt(jnp.mean(x * x, axis=-1, keepdims=True) + EPS) * w


def _silu(x):
    return x * (1.0 / (1.0 + jnp.exp(-x)))


def _softplus(x):
    return jnp.maximum(x, 0.0) + jnp.log1p(jnp.exp(-jnp.abs(x)))


def _bdot(a, b):
    return jnp.dot(a.astype(BF16), b.astype(BF16), preferred_element_type=F32)


def _bdot_nt(a, b):
    return lax.dot_general(a.astype(BF16), b.astype(BF16), (((1,), (1,)), ((), ())), preferred_element_type=F32)


def _bdot_tn(a, b):
    return lax.dot_general(a.astype(BF16), b.astype(BF16), (((0,), (0,)), ((), ())), preferred_element_type=F32)


def _fdot(a, b):
    return jnp.dot(a, b, preferred_element_type=F32, precision=HIGHEST)


def _norm_proj_kernel(x_ref, nw_ref, *refs, transposed):
    n_out = len(transposed)
    xn = _rms(x_ref[...], nw_ref[...]).astype(BF16)
    for w_ref, o_ref, tr in zip(refs[:n_out], refs[n_out:], transposed):
        if tr:
            res = lax.dot_general(w_ref[...], xn, (((1,), (1,)), ((), ())), preferred_element_type=F32)
        else:
            res = jnp.dot(xn, w_ref[...], preferred_element_type=F32)
        o_ref[...] = res.astype(o_ref.dtype)


def norm_proj(h2, nw, weights, transposed=None, tm=512):
    T, D = h2.shape
    transposed = tuple(transposed or (False,) * len(weights))
    ws = [(w.T if tr else w).astype(BF16) for w, tr in zip(weights, transposed)]
    out_specs, out_shape = [], []
    for w, tr in zip(weights, transposed):
        n = w.shape[1]
        out_specs.append(pl.BlockSpec((n, tm), lambda i: (0, i)) if tr else pl.BlockSpec((tm, n), lambda i: (i, 0)))
        out_shape.append(jax.ShapeDtypeStruct((n, T) if tr else (T, n), F32))
    return pl.pallas_call(
        functools.partial(_norm_proj_kernel, transposed=transposed),
        grid=(T // tm,),
        in_specs=[pl.BlockSpec((tm, D), lambda i: (i, 0)), _const_spec((1, D))] + [_const_spec(w.shape) for w in ws],
        out_specs=out_specs,
        out_shape=out_shape,
        compiler_params=_cparams("parallel"),
        name="norm_proj",
    )(h2, nw.reshape(1, D), *ws)


def _proj_residual_kernel(h_ref, *refs, n_in, transposed):
    acc = h_ref[...]
    for a_ref, w_ref in zip(refs[:n_in], refs[n_in:2 * n_in]):
        if transposed:
            acc = acc + _bdot_tn(a_ref[...], w_ref[...])
        else:
            acc = acc + jnp.dot(a_ref[...].astype(BF16), w_ref[...], preferred_element_type=F32)
    refs[2 * n_in][...] = acc


def proj_residual(h2, acts, weights, transposed=False, tm=512):
    T, D = h2.shape
    n_in = len(acts)
    if transposed:
        act_specs = [pl.BlockSpec((a.shape[0], tm), lambda i: (0, i)) for a in acts]
    else:
        act_specs = [pl.BlockSpec((tm, a.shape[1]), lambda i: (i, 0)) for a in acts]
    return pl.pallas_call(
        functools.partial(_proj_residual_kernel, n_in=n_in, transposed=transposed),
        grid=(T // tm,),
        in_specs=[pl.BlockSpec((tm, D), lambda i: (i, 0))] + act_specs
        + [_const_spec(w.shape) for w in weights],
        out_specs=pl.BlockSpec((tm, D), lambda i: (i, 0)),
        out_shape=jax.ShapeDtypeStruct((T, D), F32),
        compiler_params=_cparams("parallel"),
        name="proj_residual",
    )(h2, *acts, *weights)


def _ffn_kernel(h_ref, nw_ref, wg_ref, wu_ref, cwg_ref, cwu_ref, cbg_ref, cbu_ref, wd_ref, fw_ref, o_ref,
                xn_ref, acc_ref, pg_ref, pu_ref, *, n_chunks, final_norm):
    tm = h_ref.shape[0]

    @pl.when(pl.program_id(1) == 0)
    def _():
        pg_ref[...] = jnp.zeros_like(pg_ref)
        pu_ref[...] = jnp.zeros_like(pu_ref)

    x = h_ref[...]
    xn_ref[...] = _rms(x, nw_ref[...]).astype(BF16)
    acc_ref[...] = x
    row = lax.broadcasted_iota(jnp.int32, (tm, 1), 0)

    def conv(u, prev, w, b):
        m1 = jnp.where(row == 0, prev[7:8], pltpu.roll(u, 1, 0))
        m2 = jnp.where(row == 0, prev[6:7], jnp.where(row == 1, prev[7:8], pltpu.roll(u, 2, 0)))
        return u * w[2:3] + m1 * w[1:2] + m2 * w[0:1] + b

    def body(c, carry):
        xn = xn_ref[...]
        ug = jnp.dot(xn, wg_ref[c], preferred_element_type=F32)
        uu = jnp.dot(xn, wu_ref[c], preferred_element_type=F32)
        yg = conv(ug, pg_ref[c], cwg_ref[c], cbg_ref[c])
        yu = conv(uu, pu_ref[c], cwu_ref[c], cbu_ref[c])
        pg_ref[c] = ug[tm - 8:tm]
        pu_ref[c] = uu[tm - 8:tm]
        a = (_silu(yg) * yu).astype(BF16)
        acc_ref[...] += jnp.dot(a, wd_ref[c], preferred_element_type=F32)
        return carry

    lax.fori_loop(0, n_chunks, body, 0, unroll=True)
    out = acc_ref[...]
    if final_norm:
        out = _rms(out, fw_ref[...])
    o_ref[...] = out


def conv_ffn(h3, nw, w_up, conv_w, conv_b, w_down, final_w, final_norm, tm=512):
    B, S, D = h3.shape
    dff = w_down.shape[0]
    nch = dff // FF_CHUNK

    def chunked_cols(m):
        return m.reshape(m.shape[0], nch, FF_CHUNK).transpose(1, 0, 2)

    wg = chunked_cols(w_up[:, :dff]).astype(BF16)
    wu = chunked_cols(w_up[:, dff:]).astype(BF16)
    cwg = chunked_cols(conv_w[:, :dff])
    cwu = chunked_cols(conv_w[:, dff:])
    cbg = chunked_cols(conv_b[None, :dff])
    cbu = chunked_cols(conv_b[None, dff:])
    wd = w_down.reshape(nch, FF_CHUNK, D).astype(BF16)
    tile = pl.BlockSpec((None, tm, D), lambda b, s: (b, s, 0))
    return pl.pallas_call(
        functools.partial(_ffn_kernel, n_chunks=nch, final_norm=final_norm),
        grid=(B, S // tm),
        in_specs=[tile, _const_spec((1, D))] + [_const_spec(a.shape) for a in (wg, wu, cwg, cwu, cbg, cbu, wd)]
        + [_const_spec((1, D))],
        out_specs=tile,
        out_shape=jax.ShapeDtypeStruct((B, S, D), F32),
        scratch_shapes=[pltpu.VMEM((tm, D), BF16), pltpu.VMEM((tm, D), F32),
                        pltpu.VMEM((nch, 8, FF_CHUNK), F32), pltpu.VMEM((nch, 8, FF_CHUNK), F32)],
        compiler_params=_cparams("parallel", "arbitrary"),
        name="conv_ffn",
    )(h3, nw.reshape(1, D), wg, wu, cwg, cwu, cbg, cbu, wd, final_w.reshape(1, D))


def _causal_conv(x, xbuf, w, ts, width):
    xbuf[8:8 + ts, :] = x
    y = x * w[width - 1:width]
    for k in range(1, width):
        y = y + xbuf[pl.ds(8 - k, ts), :] * w[width - 1 - k:width - k]
    xbuf[0:8, :] = x[ts - 8:ts]
    return y


def _gdn_kernel(qkv_ref, sm_ref, z_ref, cw_ref, alog_ref, dtb_ref, gn_ref, o_ref,
                xbuf, qn, kn, vn, gb, state, *, ts):
    nh, dk, C = GDN_HEADS, GDN_HEAD_DIM, CHUNK
    R = nh * C

    @pl.when(pl.program_id(1) == 0)
    def _():
        xbuf[0:8, :] = jnp.zeros((8, xbuf.shape[1]), F32)
        state[...] = jnp.zeros_like(state)

    y = _silu(_causal_conv(qkv_ref[...], xbuf, cw_ref[...], ts, SHORT_CONV))
    for h in range(nh):
        qh = y[:, h * dk:(h + 1) * dk]
        qn[h] = qh * (lax.rsqrt(jnp.sum(qh * qh, axis=-1, keepdims=True) + EPS) * dk ** -0.5)
        kh = y[:, GDN_WIDTH + h * dk:GDN_WIDTH + (h + 1) * dk]
        kn[h] = kh * lax.rsqrt(jnp.sum(kh * kh, axis=-1, keepdims=True) + EPS)
        vn[h] = y[:, 2 * GDN_WIDTH + h * dk:2 * GDN_WIDTH + (h + 1) * dk]
    sm = sm_ref[...]
    lane = lax.broadcasted_iota(jnp.int32, sm.shape, 1)
    beta = 1.0 / (1.0 + jnp.exp(-sm))
    g = -jnp.exp(alog_ref[...]) * _softplus(sm + dtb_ref[...])
    gb[...] = jnp.where(lane < nh, beta, g)

    ri = lax.broadcasted_iota(jnp.int32, (R, R), 0)
    ci = lax.broadcasted_iota(jnp.int32, (R, R), 1)
    same_head = (ri >> 6) == (ci >> 6)
    causal = same_head & (ri >= ci)
    strict = same_head & (ri > ci)
    eye = (ri == ci).astype(F32)
    l64 = (lax.broadcasted_iota(jnp.int32, (C, C), 0) >= lax.broadcasted_iota(jnp.int32, (C, C), 1)).astype(F32)
    gnw = gn_ref[...]

    def stack(ref, r0):
        return jnp.concatenate([ref[h, pl.ds(r0, C), :] for h in range(nh)], axis=0)

    chunks = range(ts // C)
    each = lambda fn: [fn(c) for c in chunks]
    gbc = each(lambda c: gb[c * C:(c + 1) * C, :])
    gam = each(lambda c: _fdot(l64, gbc[c]))
    q = each(lambda c: stack(qn, c * C))
    k = each(lambda c: stack(kn, c * C))
    v = each(lambda c: stack(vn, c * C))
    beta_col = each(lambda c: jnp.concatenate([gbc[c][:, h:h + 1] for h in range(nh)], axis=0))
    gam_col = each(lambda c: jnp.concatenate([gam[c][:, nh + h:nh + h + 1] for h in range(nh)], axis=0))
    decay = each(lambda c: jnp.exp(jnp.where(
        causal, gam_col[c] - jnp.sum(eye * gam_col[c], axis=0, keepdims=True), NEG_BIG)))
    kb = each(lambda c: k[c] * beta_col[c])
    p = each(lambda c: jnp.where(strict, -(_bdot_nt(kb[c], k[c]) * decay[c]), 0.0))
    t = each(lambda c: eye + p[c])
    for _ in range(5):
        p = each(lambda c: _bdot(p[c], p[c]))
        t = each(lambda c: t[c] + _bdot(t[c], p[c]))
    eg = each(lambda c: jnp.exp(gam_col[c]))
    sol = each(lambda c: _bdot(t[c], jnp.concatenate([v[c] * beta_col[c], kb[c] * eg[c]], axis=1)))
    qk = each(lambda c: _bdot_nt(q[c], k[c]) * decay[c])
    gl = each(lambda c: [gam[c][C - 1:C, nh + h:nh + h + 1] for h in range(nh)])
    k_dec = each(lambda c: k[c] * jnp.exp(
        jnp.concatenate([jnp.broadcast_to(gl[c][h], (C, 1)) for h in range(nh)], axis=0) - gam_col[c]))
    wq = each(lambda c: [jnp.concatenate([sol[c][h * C:(h + 1) * C, dk:], (q[c] * eg[c])[h * C:(h + 1) * C]], axis=0)
                         for h in range(nh)])

    heads = range(nh)
    s_cur = [state[h] for h in heads]
    for c in chunks:
        ws = [_bdot(wq[c][h], s_cur[h]) for h in heads]
        u = [sol[c][h * C:(h + 1) * C, :dk] - ws[h][:C] for h in heads]
        o = [ws[h][C:] + _bdot(qk[c][h * C:(h + 1) * C, h * C:(h + 1) * C], u[h]) for h in heads]
        s_cur = [s_cur[h] * jnp.exp(gl[c][h]) + _bdot_tn(k_dec[c][h * C:(h + 1) * C], u[h]) for h in heads]
        for h in heads:
            zh = z_ref[c * C:(c + 1) * C, h * dk:(h + 1) * dk]
            o_ref[c * C:(c + 1) * C, h * dk:(h + 1) * dk] = _rms(o[h], gnw) * _silu(zh)
    for h in heads:
        state[h] = s_cur[h]


def gdn_heads(qkv, small, z_a, conv_w, a_log, dt_bias, gnorm, ts=256):
    B, S, _ = qkv.shape
    nh, dk = GDN_HEADS, GDN_HEAD_DIM
    alog_l = jnp.zeros((1, LANES), F32).at[0, nh:2 * nh].set(a_log)
    dtb_l = jnp.zeros((1, LANES), F32).at[0, nh:2 * nh].set(dt_bias)

    def tile(n):
        return pl.BlockSpec((None, ts, n), lambda b, s: (b, s, 0))

    return pl.pallas_call(
        functools.partial(_gdn_kernel, ts=ts),
        grid=(B, S // ts),
        in_specs=[tile(3 * GDN_WIDTH), tile(LANES), tile(GDN_WIDTH), _const_spec(conv_w.shape),
                  _const_spec((1, LANES)), _const_spec((1, LANES)), _const_spec((1, dk))],
        out_specs=tile(GDN_WIDTH),
        out_shape=jax.ShapeDtypeStruct((B, S, GDN_WIDTH), F32),
        scratch_shapes=[pltpu.VMEM((ts + 8, 3 * GDN_WIDTH), F32)]
        + [pltpu.VMEM((nh, ts, dk), F32)] * 3
        + [pltpu.VMEM((ts, LANES), F32), pltpu.VMEM((nh, dk, dk), F32)],
        compiler_params=_cparams("parallel", "arbitrary"),
        name="gdn_heads",
    )(qkv, small, z_a, conv_w, alog_l, dtb_l, gnorm.reshape(1, dk))


def _ssd_kernel(xbc_ref, sm_ref, z_ref, cw_ref, cb_ref, dtb_ref, alog_ref, d_ref, nw_ref, e_ref, o_ref,
                xbuf, xs_s, b_s, c_s, dt_s, hstate, *, ts):
    C, P, N = CHUNK, SSD_HEAD_DIM, SSD_STATE
    hpg = SSD_HEADS // SSD_GROUPS
    gw = hpg * P

    @pl.when(pl.program_id(1) == 0)
    def _():
        xbuf[0:8, :] = jnp.zeros((8, xbuf.shape[1]), F32)
        hstate[...] = jnp.zeros_like(hstate)

    y = _silu(_causal_conv(xbc_ref[...], xbuf, cw_ref[...], ts, SHORT_CONV) + cb_ref[...])
    xs_s[...] = y[:, :SSD_WIDTH]
    b_s[...] = y[:, SSD_WIDTH:SSD_WIDTH + SSD_GROUPS * N]
    c_s[...] = y[:, SSD_WIDTH + SSD_GROUPS * N:]
    sm = sm_ref[...]
    lane = lax.broadcasted_iota(jnp.int32, sm.shape, 1)
    dt = jnp.where((lane >= SSD_HEADS) & (lane < 2 * SSD_HEADS), _softplus(sm + dtb_ref[...]), 0.0)
    dt_s[...] = _fdot(dt, e_ref[...])
    a_l = -jnp.exp(alog_ref[...])
    d_l = d_ref[...]
    nw = nw_ref[...]

    li = lax.broadcasted_iota(jnp.int32, (C, C), 0)
    si = lax.broadcasted_iota(jnp.int32, (C, C), 1)
    causal = li >= si
    l64 = causal.astype(F32)
    eye = (li == si).astype(F32)

    def chunk(c, carry):
        r0 = pl.multiple_of(c * C, C)
        dtc = dt_s[pl.ds(r0, C), :]
        xs = xs_s[pl.ds(r0, C), :]
        xc = xs * dtc
        acs = _fdot(l64, dtc * a_l)
        acs_last = acs[C - 1:C, :]
        eacs = jnp.exp(acs)
        xd = xc * jnp.exp(acs_last - acs)
        e_last = jnp.exp(acs_last)
        ys = []
        for g in range(SSD_GROUPS):
            gl = slice(g * gw, (g + 1) * gw)
            bg = b_s[pl.ds(r0, C), g * N:(g + 1) * N]
            cg = c_s[pl.ds(r0, C), g * N:(g + 1) * N]
            cb = _bdot_nt(cg, bg)
            h_in = hstate[g]
            y_off = _bdot(cg, h_in) * eacs[:, gl]
            yd = []
            for j in range(hpg):
                hl = slice((g * hpg + j) * P, (g * hpg + j + 1) * P)
                blk = acs[:, hl]
                row = jnp.sum(blk * eye, axis=0, keepdims=True)
                m = cb * jnp.exp(jnp.where(causal, blk - row, NEG_BIG))
                yd.append(_bdot(m, xc[:, hl]))
            hstate[g] = h_in * e_last[:, gl] + _bdot_tn(bg, xd[:, gl])
            ys.append(jnp.concatenate(yd, axis=1) + y_off)
        yy = jnp.concatenate(ys, axis=1) + d_l * xs
        yy = yy * _silu(z_ref[pl.ds(r0, C), :])
        o_ref[pl.ds(r0, C), :] = jnp.concatenate(
            [_rms(yy[:, g * gw:(g + 1) * gw], nw[:, g * gw:(g + 1) * gw]) for g in range(SSD_GROUPS)], axis=1)
        return carry

    lax.fori_loop(0, ts // C, chunk, 0, unroll=True)


def ssd_heads(xbc, small, z_b, conv_w, conv_b, a_log, dt_bias, d_skip, norm_w, ts=256):
    B, S, _ = xbc.shape
    nh, P = SSD_HEADS, SSD_HEAD_DIM
    dtb_l = jnp.zeros((1, LANES), F32).at[0, nh:2 * nh].set(dt_bias)
    expand = np.zeros((LANES, SSD_WIDTH), np.float32)
    for h in range(nh):
        expand[nh + h, h * P:(h + 1) * P] = 1.0

    def tile(n):
        return pl.BlockSpec((None, ts, n), lambda b, s: (b, s, 0))

    gw = SSD_WIDTH // SSD_GROUPS
    return pl.pallas_call(
        functools.partial(_ssd_kernel, ts=ts),
        grid=(B, S // ts),
        in_specs=[tile(SSD_XBC), tile(LANES), tile(SSD_WIDTH), _const_spec(conv_w.shape), _const_spec((1, SSD_XBC)),
                  _const_spec((1, LANES)), _const_spec((1, SSD_WIDTH)), _const_spec((1, SSD_WIDTH)),
                  _const_spec((1, SSD_WIDTH)), _const_spec((LANES, SSD_WIDTH))],
        out_specs=tile(SSD_WIDTH),
        out_shape=jax.ShapeDtypeStruct((B, S, SSD_WIDTH), F32),
        scratch_shapes=[pltpu.VMEM((ts + 8, SSD_XBC), F32), pltpu.VMEM((ts, SSD_WIDTH), F32),
                        pltpu.VMEM((ts, SSD_GROUPS * SSD_STATE), F32), pltpu.VMEM((ts, SSD_GROUPS * SSD_STATE), F32),
                        pltpu.VMEM((ts, SSD_WIDTH), F32), pltpu.VMEM((SSD_GROUPS, SSD_STATE, gw), F32)],
        compiler_params=_cparams("parallel", "arbitrary"),
        name="ssd_heads",
    )(xbc, small, z_b, conv_w, conv_b.reshape(1, SSD_XBC), dtb_l, jnp.repeat(a_log, P).reshape(1, SSD_WIDTH),
      jnp.repeat(d_skip, P).reshape(1, SSD_WIDTH), norm_w.reshape(1, SSD_WIDTH), jnp.asarray(expand))


def hybrid_layer(h, nw, w_in, gdn_conv, gdn_a_log, gdn_dt_bias, gdn_norm, ssd_conv, ssd_conv_bias, ssd_a_log,
                 ssd_dt_bias, ssd_d, ssd_norm, w_out):
    B, S, D = h.shape
    T = B * S
    gwid, swid = GDN_WIDTH, SSD_WIDTH
    o_za = 3 * gwid
    o_ba = o_za + gwid
    o_zb = o_ba + 2 * GDN_HEADS
    o_xbc = o_zb + swid
    o_dt = o_xbc + SSD_XBC
    w_small = jnp.zeros((D, LANES), F32)
    w_small = w_small.at[:, :2 * GDN_HEADS].set(w_in[:, o_ba:o_zb])
    w_small = w_small.at[:, SSD_HEADS:2 * SSD_HEADS].set(w_in[:, o_dt:o_dt + SSD_HEADS])
    ws = [w_in[:, :o_za], w_in[:, o_za:o_ba], w_small, w_in[:, o_zb:o_xbc], w_in[:, o_xbc:o_dt]]
    h2 = h.reshape(T, D)
    qkv, z_a, small, z_b, xbc = norm_proj(h2, nw, [w.astype(BF16) for w in ws])
    small3 = small.reshape(B, S, LANES)
    o_a = gdn_heads(qkv.reshape(B, S, -1), small3, z_a.reshape(B, S, -1), gdn_conv, gdn_a_log, gdn_dt_bias, gdn_norm)
    y_b = ssd_heads(xbc.reshape(B, S, -1), small3, z_b.reshape(B, S, -1), ssd_conv, ssd_conv_bias, ssd_a_log,
                    ssd_dt_bias, ssd_d, ssd_norm)
    out = proj_residual(h2, [o_a.reshape(T, gwid), y_b.reshape(T, swid)],
                        [w_out[:gwid].astype(BF16), w_out[gwid:].astype(BF16)])
    return out.reshape(B, S, D)


def _compress_kernel(kc_ref, vc_ref, pk0_ref, pk1_ref, pv0_ref, pv1_ref, w1ka_ref, w1kb_ref, w1va_ref, w1vb_ref,
                     w2k_ref, w2v_ref, ko_ref, vo_ref):
    n_seg = kc_ref.shape[1]
    for src, p0, p1, wa, wb, w2, out in ((kc_ref, pk0_ref, pk1_ref, w1ka_ref, w1kb_ref, w2k_ref, ko_ref),
                                         (vc_ref, pv0_ref, pv1_ref, w1va_ref, w1vb_ref, w2v_ref, vo_ref)):
        for g in range(NSA_KV_GROUPS):
            t = src[g]
            f0 = _bdot(t + p0[...], wa[...])
            f1 = _bdot(t + p1[...], wb[...])
            hid = f0 + pltpu.roll(f1, n_seg - 1, 0)
            out[g] = _bdot(_silu(hid), w2[...])


def nsa_compress(kc_r, vc_r, pos_k, w1_k, w2_k, pos_v, w1_v, w2_v):
    B, G, n_seg, seg_w = kc_r.shape
    half = CMP_STRIDE
    args = [kc_r, vc_r,
            pos_k[:half].reshape(1, seg_w), pos_k[half:].reshape(1, seg_w),
            pos_v[:half].reshape(1, seg_w), pos_v[half:].reshape(1, seg_w),
            w1_k[:seg_w].astype(BF16), w1_k[seg_w:].astype(BF16), w1_v[:seg_w].astype(BF16), w1_v[seg_w:].astype(BF16),
            w2_k.astype(BF16), w2_v.astype(BF16)]
    per_b = pl.BlockSpec((None, G, n_seg, seg_w), lambda b: (b, 0, 0, 0))
    out_b = pl.BlockSpec((None, G, n_seg, NSA_HEAD_DIM), lambda b: (b, 0, 0, 0))
    return pl.pallas_call(
        _compress_kernel,
        grid=(B,),
        in_specs=[per_b, per_b] + [_const_spec(a.shape) for a in args[2:]],
        out_specs=[out_b, out_b],
        out_shape=[jax.ShapeDtypeStruct((B, G, n_seg, NSA_HEAD_DIM), F32)] * 2,
        compiler_params=_cparams("parallel"),
        name="nsa_compress",
    )(*args)


def _alibi_slopes():
    return [[float(2.0 ** (-8.0 * (g * NSA_J + j + 1) / NSA_HEADS)) for j in range(NSA_J)]
            for g in range(NSA_KV_GROUPS)]


M_INIT = -1e20


def _softmax_step(carry, s, pen, v):
    m, l, acc = carry
    s = s - pen
    m_new = jnp.maximum(m, jnp.max(s, axis=-1, keepdims=True))
    alpha = jnp.exp(m - m_new)
    e = jnp.exp(s - m_new)
    l = alpha * l + jnp.sum(e, axis=-1, keepdims=True)
    acc = alpha * acc + _bdot(e, v)
    return m_new, l, acc


def _nsa_kernel(q_ref, gt_ref, kc_ref, vc_ref, ks_ref, vs_ref, kw_ref, vw_ref, cov_ref, o_ref,
                *, tq, tk, seq, n_cmp, n_sel):
    G, J, Dh = NSA_KV_GROUPS, NSA_J, NSA_HEAD_DIM
    slopes = _alibi_slopes()
    start = pl.program_id(1) * tq
    t_col = start + lax.broadcasted_iota(jnp.int32, (tq, 1), 0)
    gates = 1.0 / (1.0 + jnp.exp(-gt_ref[...]))
    nc = kc_ref.shape[1]
    rows = J * tq
    top_k = min(SEL_TOPK, n_sel)

    def init():
        return (jnp.full((rows, 1), M_INIT, F32), jnp.zeros((rows, 1), F32), jnp.zeros((rows, Dh), F32))

    def penalty(dist, valid, g):
        distf = dist.astype(F32)
        mask_pen = jnp.where(valid, 0.0, -NEG_BIG)
        return jnp.concatenate([slopes[g][j] * distf + mask_pen for j in range(J)], axis=0)

    def finish(carry):
        _, l, acc = carry
        return acc / jnp.maximum(l, 1e-30)

    for g in range(G):
        qg = jnp.concatenate([q_ref[:, (g * J + j) * Dh:(g * J + j + 1) * Dh] for j in range(J)], axis=0) * Dh ** -0.5

        n_idx = lax.broadcasted_iota(jnp.int32, (tq, nc), 1)
        dist_c = t_col - (n_idx * CMP_STRIDE + (CMP_BLOCK - 1))
        valid_c = (dist_c >= 0) & (n_idx < n_cmp)
        s_c = lax.dot_general(qg, kc_ref[g], (((1,), (1,)), ((), ())), preferred_element_type=F32, precision=HIGHEST)
        s_c = s_c - penalty(dist_c, valid_c, g)
        m_c = jnp.maximum(jnp.max(s_c, axis=-1, keepdims=True), M_INIT)
        e_c = jnp.exp(s_c - m_c)
        p_c = e_c / jnp.maximum(jnp.sum(e_c, axis=-1, keepdims=True), 1e-30)
        o_c = _bdot(p_c, vc_ref[g])
        p_sum = p_c[0:tq]
        for j in range(1, J):
            p_sum = p_sum + p_c[j * tq:(j + 1) * tq]
        imp = _fdot(p_sum, cov_ref[...])

        blk = lax.broadcasted_iota(jnp.int32, (tq, LANES), 1)
        cur = t_col >> 6
        forced = (blk == 0) | (blk == cur) | (blk == cur - 1)
        imp = jnp.where(forced, SEL_FORCE, jnp.where(blk * SEL_BLOCK <= t_col, imp, -SEL_FORCE))
        imp = jnp.where(blk < n_sel, imp, -3e38)
        sel = jnp.zeros((tq, LANES), F32)
        for _ in range(top_k):
            mx = jnp.max(imp, axis=-1, keepdims=True)
            first = jnp.min(jnp.where(imp == mx, blk, LANES), axis=-1, keepdims=True)
            hit = blk == first
            sel = jnp.where(hit, 1.0, sel)
            imp = jnp.where(hit, -jnp.inf, imp)

        def sel_step(kt, carry):
            k0 = pl.multiple_of(kt * tk, tk)
            pos = k0 + lax.broadcasted_iota(jnp.int32, (tq, tk), 1)
            e_blk = lax.broadcasted_iota(jnp.int32, (LANES, tk), 0)
            e_pos = k0 + lax.broadcasted_iota(jnp.int32, (LANES, tk), 1)
            expand = ((e_pos >> 6) == e_blk).astype(BF16)
            chosen = jnp.dot(sel.astype(BF16), expand, preferred_element_type=F32) > 0.5
            dist = t_col - pos
            s = _bdot_nt(qg, ks_ref[pl.ds(k0, tk), g * Dh:(g + 1) * Dh])
            return _softmax_step(carry, s, penalty(dist, chosen & (dist >= 0), g),
                                 vs_ref[pl.ds(k0, tk), g * Dh:(g + 1) * Dh])

        n_kt = (start + tq + tk - 1) // tk
        o_s = finish(lax.fori_loop(0, n_kt, sel_step, init()))

        band0 = jnp.clip(start - WINDOW, 0, seq - (WINDOW + tq))

        def win_step(kt, carry):
            k0 = pl.multiple_of(band0 + kt * tq, tq)
            pos = k0 + lax.broadcasted_iota(jnp.int32, (tq, tq), 1)
            dist = t_col - pos
            s = _bdot_nt(qg, kw_ref[pl.ds(k0, tq), g * Dh:(g + 1) * Dh])
            return _softmax_step(carry, s, penalty(dist, (dist >= 0) & (dist < WINDOW), g),
                                 vw_ref[pl.ds(k0, tq), g * Dh:(g + 1) * Dh])

        o_w = finish(lax.fori_loop(0, (WINDOW + tq) // tq, win_step, init()))

        outs = []
        for j in range(J):
            r = slice(j * tq, (j + 1) * tq)
            col = g * J + j
            outs.append(gates[:, col:col + 1] * o_c[r] + gates[:, NSA_HEADS + col:NSA_HEADS + col + 1] * o_s[r]
                        + gates[:, 2 * NSA_HEADS + col:2 * NSA_HEADS + col + 1] * o_w[r])
        o_ref[:, g * J * Dh:(g + 1) * J * Dh] = jnp.concatenate(outs, axis=1)


def nsa_attention(q, gates, k_cmp, v_cmp, k_s, v_s, k_w, v_w, tq=128, tk=256):
    B, S, _ = q.shape
    n_seg = k_cmp.shape[2]
    n_cmp = (S - CMP_BLOCK) // CMP_STRIDE + 1
    n_sel = S // SEL_BLOCK
    cs = np.arange(n_seg) * CMP_STRIDE
    ss = np.arange(n_sel) * SEL_BLOCK
    cover = np.clip(np.minimum(cs[:, None] + CMP_BLOCK, ss[None, :] + SEL_BLOCK)
                    - np.maximum(cs[:, None], ss[None, :]), 0, None) / CMP_STRIDE
    cover_pad = np.zeros((n_seg, LANES), np.float32)
    cover_pad[:n_cmp, :n_sel] = cover[:n_cmp]
    width = NSA_HEADS * NSA_HEAD_DIM

    def tile(n):
        return pl.BlockSpec((None, tq, n), lambda b, s: (b, s, 0))

    per_b_cmp = pl.BlockSpec((None, NSA_KV_GROUPS, n_seg, NSA_HEAD_DIM), lambda b, s: (b, 0, 0, 0))
    per_b_seq = pl.BlockSpec((None, S, NSA_KV), lambda b, s: (b, 0, 0))
    return pl.pallas_call(
        functools.partial(_nsa_kernel, tq=tq, tk=tk, seq=S, n_cmp=n_cmp, n_sel=n_sel),
        grid=(B, S // tq),
        in_specs=[tile(width), tile(LANES), per_b_cmp, per_b_cmp, per_b_seq, per_b_seq, per_b_seq, per_b_seq,
                  _const_spec((n_seg, LANES))],
        out_specs=tile(width),
        out_shape=jax.ShapeDtypeStruct((B, S, width), F32),
        compiler_params=_cparams("parallel", "arbitrary"),
        name="nsa_attention",
    )(q, gates, k_cmp, v_cmp, k_s, v_s, k_w, v_w, jnp.asarray(cover_pad))


TQ = LANES
BF16_SUBLANES = 16
AUG = 128
POS_ROW0 = NSA_HEAD_DIM
SEL_ROW0 = POS_ROW0 + BF16_SUBLANES
ZERO_ROWS = BF16_SUBLANES
V_ROWS = NSA_HEAD_DIM + BF16_SUBLANES
POS_BITS = 6
N_SPLIT = 5
SEL_SHIFT = SEL_BLOCK.bit_length() - 1
LOG2E = math.log2(math.e)


def _slope_rows():
    G, J = NSA_KV_GROUPS, NSA_J
    rows = np.zeros((G, SEL_ROW0 - POS_ROW0, J * TQ), np.float32)
    for g, per_group in enumerate(_alibi_slopes()):
        for j, slope in enumerate(per_group):
            rest = float(np.float32(slope)) * LOG2E
            for i in range(N_SPLIT):
                piece = float(np.asarray(rest, dtype=BF16))
                rows[g, i, j * TQ:(j + 1) * TQ] = piece * 2.0 ** POS_BITS
                rows[g, N_SPLIT + i, j * TQ:(j + 1) * TQ] = piece
                rest -= piece
    return jnp.asarray(rows, BF16)


def _nsa_t_kernel(qT_ref, gT_ref, kc_ref, vc_ref, ks_ref, vsT_ref, kw_ref, vwT_ref, covT_ref, slope_ref, o_ref,
                  kaug_s, kaug_w, vaug_s, vaug_w, qaug, oc_ref, m_ref, acc_ref, *, tk, seq, n_cmp, n_sel):
    G, J, Dh, tq = NSA_KV_GROUPS, NSA_J, NSA_HEAD_DIM, TQ
    slopes = _alibi_slopes()
    qi = pl.program_id(1)
    start = qi * tq
    nc = kc_ref.shape[1]
    nsp = AUG - SEL_ROW0 - ZERO_ROWS
    top_k = min(SEL_TOPK, n_sel)
    lanes = J * tq

    @pl.when(qi == 0)
    def _():
        pos = lax.broadcasted_iota(jnp.int32, (seq, AUG - Dh), 0)
        col = lax.broadcasted_iota(jnp.int32, (seq, AUG - Dh), 1) + Dh
        pos_cols = jnp.where(col < POS_ROW0 + N_SPLIT, pos >> POS_BITS,
                             jnp.where(col < POS_ROW0 + 2 * N_SPLIT, pos & (2 ** POS_BITS - 1), 0)).astype(F32)
        onehot = ((col >= SEL_ROW0) & ((pos >> SEL_SHIFT) == col - SEL_ROW0)).astype(F32)
        ones_row = (lax.broadcasted_iota(jnp.int32, (V_ROWS - Dh, seq), 0) == 0).astype(F32)
        for g in range(G):
            for k_ref, ka, vT_ref, va, extra in ((ks_ref, kaug_s, vsT_ref, vaug_s, pos_cols + onehot),
                                                 (kw_ref, kaug_w, vwT_ref, vaug_w, pos_cols)):
                ka[g] = jnp.concatenate([k_ref[:, g * Dh:(g + 1) * Dh], extra], axis=1).astype(BF16)
                va[g] = jnp.concatenate([vT_ref[g * Dh:(g + 1) * Dh, :], ones_row], axis=0).astype(BF16)

    t_row = start + lax.broadcasted_iota(jnp.int32, (1, tq), 1)
    gates = 1.0 / (1.0 + jnp.exp(-gT_ref[...]))

    def per_head(x, fn):
        return jnp.concatenate([fn(j, x[:, j * tq:(j + 1) * tq]) for j in range(J)], axis=1)

    for g in range(G):
        qT = jnp.concatenate([qT_ref[(g * J + j) * Dh:(g * J + j + 1) * Dh, :] for j in range(J)], axis=1) * Dh ** -0.5

        n_idx = lax.broadcasted_iota(jnp.int32, (nc, tq), 0)
        dist_c = t_row - (n_idx * CMP_STRIDE + (CMP_BLOCK - 1))
        pen_c = jnp.where((dist_c >= 0) & (n_idx < n_cmp), 0.0, -NEG_BIG)
        distf_c = dist_c.astype(F32)
        s_c = per_head(_fdot(kc_ref[g], qT), lambda j, sj: sj - (slopes[g][j] * distf_c + pen_c))
        m_c = jnp.maximum(jnp.max(s_c, axis=0, keepdims=True), M_INIT)
        e_c = jnp.exp(s_c - m_c)
        p_c = e_c / jnp.maximum(jnp.sum(e_c, axis=0, keepdims=True), 1e-30)
        oc_ref[g] = _bdot_tn(vc_ref[g], p_c)
        p_sum = p_c[:, 0:tq]
        for j in range(1, J):
            p_sum = p_sum + p_c[:, j * tq:(j + 1) * tq]
        imp = _fdot(covT_ref[...], p_sum)

        blk = lax.broadcasted_iota(jnp.int32, (nsp, tq), 0)
        cur = t_row >> SEL_SHIFT
        forced = (blk == 0) | (blk == cur) | (blk == cur - 1)
        imp = jnp.where(forced, SEL_FORCE, jnp.where(blk * SEL_BLOCK <= t_row, imp, -SEL_FORCE))
        imp = jnp.where(blk < n_sel, imp, -3e38)
        sel_pen = jnp.full((nsp, tq), NEG_BIG, F32)
        for _ in range(top_k):
            mx = jnp.max(imp, axis=0, keepdims=True)
            first = jnp.min(jnp.where(imp == mx, blk, nsp), axis=0, keepdims=True)
            hit = blk == first
            sel_pen = jnp.where(hit, 0.0, sel_pen)
            imp = jnp.where(hit, -jnp.inf, imp)

        qaug[g] = jnp.concatenate(
            [(qT * LOG2E).astype(BF16), slope_ref[g], jnp.concatenate([sel_pen] * J, axis=1).astype(BF16),
             jnp.zeros((ZERO_ROWS, lanes), BF16)], axis=0)

    def attend(tile, k0_of, kaug, vaug):
        m_ref[...] = jnp.full(m_ref.shape, M_INIT, F32)
        acc_ref[...] = jnp.zeros(acc_ref.shape, F32)

        def scores(kt):
            k0 = pl.multiple_of(k0_of(kt), tile)
            return [jnp.dot(kaug[g, pl.ds(k0, tile), :], qaug[g], preferred_element_type=F32) for g in range(G)]

        def update(kt, sc, valid_fn):
            k0 = pl.multiple_of(k0_of(kt), tile)
            if valid_fn is not None:
                pos = k0 + lax.broadcasted_iota(jnp.int32, (tile, tq), 0)
                mask_pen = jnp.where(valid_fn(t_row - pos), 0.0, NEG_BIG)
            for g in range(G):
                s = sc[g]
                if valid_fn is not None:
                    s = per_head(s, lambda j, sj: sj + mask_pen)
                m_old = m_ref[g]
                m_new = jnp.maximum(m_old, jnp.max(s, axis=0, keepdims=True))
                e = jnp.exp2(s - m_new).astype(BF16)
                pv = jnp.dot(vaug[g, :, pl.ds(k0, tile)], e, preferred_element_type=F32)
                acc_ref[g] = jnp.exp2(m_old - m_new) * acc_ref[g] + pv
                m_ref[g] = m_new

        return scores, update

    def finish(g):
        acc = acc_ref[g]
        return acc[0:Dh] / jnp.maximum(acc[Dh:Dh + 1], 1e-30)

    n_past = (start + tq + tk - 1) // tk - 1
    sel_scores, sel_update = attend(tk, lambda kt: kt * tk, kaug_s, vaug_s)
    diag_valid = lambda dist: dist >= 0

    def past_pair(p, carry):
        s0 = sel_scores(2 * p)
        s1 = sel_scores(2 * p + 1)
        sel_update(2 * p, s0, None)
        sel_update(2 * p + 1, s1, None)
        return carry

    lax.fori_loop(0, n_past // 2, past_pair, 0)

    @pl.when(n_past % 2 == 1)
    def _():
        s0 = sel_scores(n_past - 1)
        s1 = sel_scores(n_past)
        sel_update(n_past - 1, s0, None)
        sel_update(n_past, s1, diag_valid)

    @pl.when(n_past % 2 == 0)
    def _():
        sel_update(n_past, sel_scores(n_past), diag_valid)

    o_s = [finish(g) for g in range(G)]

    band0 = jnp.clip(start - WINDOW, 0, seq - (WINDOW + tq))
    win_scores, win_update = attend(tq, lambda kt: band0 + kt * tq, kaug_w, vaug_w)
    n_win = (WINDOW + tq) // tq
    sc_next = win_scores(0)
    for kt in range(n_win):
        sc = sc_next
        if kt + 1 < n_win:
            sc_next = win_scores(kt + 1)
        win_update(kt, sc, lambda dist: (dist >= 0) & (dist < WINDOW))
    o_w = [finish(g) for g in range(G)]

    for g in range(G):
        def gate_row(branch):
            return jnp.concatenate([gates[branch * NSA_HEADS + g * J + j:branch * NSA_HEADS + g * J + j + 1, :]
                                    for j in range(J)], axis=1)

        out = gate_row(0) * oc_ref[g] + gate_row(1) * o_s[g] + gate_row(2) * o_w[g]
        for j in range(J):
            o_ref[(g * J + j) * Dh:(g * J + j + 1) * Dh, :] = out[:, j * tq:(j + 1) * tq].astype(o_ref.dtype)


def _nsa_t_kernel_old(qT_ref, gT_ref, kc_ref, vc_ref, ks_ref, vsT_ref, kw_ref, vwT_ref, covT_ref, slope_ref, o_ref,
                  kaug_s, kaug_w, vaug_s, vaug_w, selT, m_ref, acc_ref, *, tk, seq, n_cmp, n_sel):
    G, J, Dh, tq = NSA_KV_GROUPS, NSA_J, NSA_HEAD_DIM, TQ
    slopes = _alibi_slopes()
    qi = pl.program_id(1)
    start = qi * tq
    nc = kc_ref.shape[1]
    nsp = selT.shape[0]
    top_k = min(SEL_TOPK, n_sel)

    @pl.when(qi == 0)
    def _():
        pos = lax.broadcasted_iota(jnp.int32, (seq, AUG - Dh), 0)
        col = lax.broadcasted_iota(jnp.int32, (seq, AUG - Dh), 1)
        extra = jnp.where(col < N_SPLIT, pos >> POS_BITS,
                          jnp.where(col < 2 * N_SPLIT, pos & (2 ** POS_BITS - 1), 0)).astype(F32)
        ones_row = (lax.broadcasted_iota(jnp.int32, (V_ROWS - Dh, seq), 0) == 0).astype(F32)
        for g in range(G):
            for k_ref, ka, vT_ref, va in ((ks_ref, kaug_s, vsT_ref, vaug_s), (kw_ref, kaug_w, vwT_ref, vaug_w)):
                ka[g] = jnp.concatenate([k_ref[:, g * Dh:(g + 1) * Dh], extra], axis=1).astype(BF16)
                va[g] = jnp.concatenate([vT_ref[g * Dh:(g + 1) * Dh, :], ones_row], axis=0).astype(BF16)

    t_row = start + lax.broadcasted_iota(jnp.int32, (1, tq), 1)
    gates = 1.0 / (1.0 + jnp.exp(-gT_ref[...]))

    def per_head(x, fn):
        return jnp.concatenate([fn(j, x[:, j * tq:(j + 1) * tq]) for j in range(J)], axis=1)

    def attend(n_tiles, tile, k0_of, kaug, vaug, g, qaug, valid_fn):
        m_ref[...] = jnp.full(m_ref.shape, M_INIT, F32)
        acc_ref[...] = jnp.zeros(acc_ref.shape, F32)

        def step(kt, carry):
            k0 = pl.multiple_of(k0_of(kt), tile)
            pos = k0 + lax.broadcasted_iota(jnp.int32, (tile, tq), 0)
            mask_pen = jnp.where(valid_fn(kt, t_row - pos), 0.0, NEG_BIG)
            s = jnp.dot(kaug[g, pl.ds(k0, tile), :], qaug, preferred_element_type=F32)
            s = per_head(s, lambda j, sj: sj + mask_pen)
            m_old = m_ref[...]
            m_new = jnp.maximum(m_old, jnp.max(s, axis=0, keepdims=True))
            e = jnp.exp(s - m_new).astype(BF16)
            pv = jnp.dot(vaug[g, :, pl.ds(k0, tile)], e, preferred_element_type=F32)
            acc_ref[...] = jnp.exp(m_old - m_new) * acc_ref[...] + pv
            m_ref[...] = m_new
            return carry

        if isinstance(n_tiles, int):
            lax.fori_loop(0, n_tiles, step, 0, unroll=True)
        else:
            def pair(p, carry):
                return step(2 * p + 1, step(2 * p, carry))

            lax.fori_loop(0, n_tiles // 2, pair, 0)

            @pl.when(n_tiles % 2 == 1)
            def _():
                step(n_tiles - 1, 0)

        acc = acc_ref[...]
        return acc[0:Dh] / jnp.maximum(acc[Dh:Dh + 1], 1e-30)

    for g in range(G):
        qT = jnp.concatenate([qT_ref[(g * J + j) * Dh:(g * J + j + 1) * Dh, :] for j in range(J)], axis=1) * Dh ** -0.5

        n_idx = lax.broadcasted_iota(jnp.int32, (nc, tq), 0)
        dist_c = t_row - (n_idx * CMP_STRIDE + (CMP_BLOCK - 1))
        pen_c = jnp.where((dist_c >= 0) & (n_idx < n_cmp), 0.0, -NEG_BIG)
        distf_c = dist_c.astype(F32)
        s_c = per_head(_fdot(kc_ref[g], qT), lambda j, sj: sj - (slopes[g][j] * distf_c + pen_c))
        m_c = jnp.maximum(jnp.max(s_c, axis=0, keepdims=True), M_INIT)
        e_c = jnp.exp(s_c - m_c)
        p_c = e_c / jnp.maximum(jnp.sum(e_c, axis=0, keepdims=True), 1e-30)
        o_c = _bdot_tn(vc_ref[g], p_c)
        p_sum = p_c[:, 0:tq]
        for j in range(1, J):
            p_sum = p_sum + p_c[:, j * tq:(j + 1) * tq]
        imp = _fdot(covT_ref[...], p_sum)

        blk = lax.broadcasted_iota(jnp.int32, (nsp, tq), 0)
        cur = t_row >> 6
        forced = (blk == 0) | (blk == cur) | (blk == cur - 1)
        imp = jnp.where(forced, SEL_FORCE, jnp.where(blk * SEL_BLOCK <= t_row, imp, -SEL_FORCE))
        imp = jnp.where(blk < n_sel, imp, -3e38)
        sel = jnp.zeros((nsp, tq), F32)
        for _ in range(top_k):
            mx = jnp.max(imp, axis=0, keepdims=True)
            first = jnp.min(jnp.where(imp == mx, blk, nsp), axis=0, keepdims=True)
            hit = blk == first
            sel = jnp.where(hit, 1.0, sel)
            imp = jnp.where(hit, -jnp.inf, imp)
        selT[...] = sel

        qaug = jnp.concatenate([qT.astype(BF16), slope_ref[g]], axis=0)

        bpt = tk // SEL_BLOCK

        def sel_valid(kt, dist):
            chosen = jnp.concatenate(
                [jnp.broadcast_to(selT[pl.ds(kt * bpt + i, 1), :], (SEL_BLOCK, tq)) for i in range(bpt)], axis=0)
            return (chosen > 0.5) & (dist >= 0)

        o_s = attend((start + tq + tk - 1) // tk, tk, lambda kt: kt * tk, kaug_s, vaug_s, g, qaug, sel_valid)

        band0 = jnp.clip(start - WINDOW, 0, seq - (WINDOW + tq))
        o_w = attend((WINDOW + tq) // tq, tq, lambda kt: band0 + kt * tq, kaug_w, vaug_w, g, qaug,
                     lambda kt, dist: (dist >= 0) & (dist < WINDOW))

        def gate_row(branch):
            return jnp.concatenate([gates[branch * NSA_HEADS + g * J + j:branch * NSA_HEADS + g * J + j + 1, :]
                                    for j in range(J)], axis=1)

        out = gate_row(0) * o_c + gate_row(1) * o_s + gate_row(2) * o_w
        for j in range(J):
            o_ref[(g * J + j) * Dh:(g * J + j + 1) * Dh, :] = out[:, j * tq:(j + 1) * tq].astype(o_ref.dtype)


def nsa_attention_t(qT, gT, k_cmp, v_cmp, k_s, vT_s, k_w, vT_w, B, S, tk=256):
    width, T = qT.shape
    tq = TQ
    n_seg = k_cmp.shape[2]
    n_cmp = (S - CMP_BLOCK) // CMP_STRIDE + 1
    n_sel = S // SEL_BLOCK
    nsp = AUG - SEL_ROW0 - ZERO_ROWS
    assert n_sel <= nsp and S >= WINDOW + tq and POS_ROW0 + 2 * N_SPLIT <= SEL_ROW0
    cs = np.arange(n_seg) * CMP_STRIDE
    ss = np.arange(n_sel) * SEL_BLOCK
    cover = np.clip(np.minimum(cs[:, None] + CMP_BLOCK, ss[None, :] + SEL_BLOCK)
                    - np.maximum(cs[:, None], ss[None, :]), 0, None) / CMP_STRIDE
    cover_t = np.zeros((nsp, n_seg), np.float32)
    cover_t[:n_sel, :n_cmp] = cover[:n_cmp].T
    slope_rows = _slope_rows()
    nq = S // tq

    def qtile(n):
        return pl.BlockSpec((n, tq), lambda b, s: (0, b * nq + s))

    per_b_cmp = pl.BlockSpec((None, NSA_KV_GROUPS, n_seg, NSA_HEAD_DIM), lambda b, s: (b, 0, 0, 0))
    per_b_k = pl.BlockSpec((S, NSA_KV), lambda b, s: (b, 0))
    per_b_vT = pl.BlockSpec((NSA_KV, S), lambda b, s: (0, b))
    G = NSA_KV_GROUPS
    return pl.pallas_call(
        functools.partial(_nsa_t_kernel, tk=tk, seq=S, n_cmp=n_cmp, n_sel=n_sel),
        grid=(B, nq),
        in_specs=[qtile(width), qtile(LANES), per_b_cmp, per_b_cmp, per_b_k, per_b_vT, per_b_k, per_b_vT,
                  _const_spec(cover_t.shape), _const_spec(slope_rows.shape)],
        out_specs=qtile(width),
        out_shape=jax.ShapeDtypeStruct((width, T), BF16),
        scratch_shapes=[pltpu.VMEM((G, S, AUG), BF16), pltpu.VMEM((G, S, AUG), BF16),
                        pltpu.VMEM((G, V_ROWS, S), BF16), pltpu.VMEM((G, V_ROWS, S), BF16),
                        pltpu.VMEM((G, AUG, NSA_J * tq), BF16), pltpu.VMEM((G, NSA_HEAD_DIM, NSA_J * tq), F32),
                        pltpu.VMEM((G, 1, NSA_J * tq), F32), pltpu.VMEM((G, V_ROWS, NSA_J * tq), F32)],
        compiler_params=_cparams("parallel", "arbitrary"),
        name="nsa_attention",
    )(qT, gT, k_cmp, v_cmp, k_s, vT_s, k_w, vT_w, jnp.asarray(cover_t), slope_rows)


def nsa_layer(h, nw, w_in, cmp_pos_k, cmp_w1_k, cmp_w2_k, cmp_pos_v, cmp_w1_v, cmp_w2_v, w_out):
    B, S, D = h.shape
    T = B * S
    width = NSA_HEADS * NSA_HEAD_DIM
    G, Dh = NSA_KV_GROUPS, NSA_HEAD_DIM
    w_gates = jnp.zeros((D, LANES), F32).at[:, :3 * NSA_HEADS].set(w_in[:, width + 6 * NSA_KV:])
    ws = [w_in[:, :width]] + [w_in[:, width + i * NSA_KV:width + (i + 1) * NSA_KV] for i in range(6)] + [w_gates]
    h2 = h.reshape(T, D)
    qT, k_c, v_c, k_s, vT_s, k_w, vT_w, gT = norm_proj(
        h2, nw, ws, transposed=(True, False, False, False, True, False, True, True))
    n_seg = S // CMP_STRIDE

    def segments(t):
        return t.reshape(B, S, G, Dh).transpose(0, 2, 1, 3).reshape(B, G, n_seg, CMP_STRIDE * Dh)

    k_cmp, v_cmp = nsa_compress(segments(k_c), segments(v_c), cmp_pos_k, cmp_w1_k, cmp_w2_k,
                                cmp_pos_v, cmp_w1_v, cmp_w2_v)
    oT = nsa_attention_t(qT, gT, k_cmp, v_cmp, k_s, vT_s, k_w, vT_w, B, S)
    out = proj_residual(h2, [oT], [w_out.astype(BF16)], transposed=True)
    return out.reshape(B, S, D)


def kernel(x, norm_mix, norm_ffn, norm_final, hy_w_in, gdn_conv, gdn_a_log, gdn_dt_bias, gdn_norm, ssd_conv, ssd_conv_bias, ssd_a_log, ssd_dt_bias, ssd_d, ssd_norm, hy_w_out, nsa_w_in, cmp_pos_k, cmp_w1_k, cmp_w2_k, cmp_pos_v, cmp_w1_v, cmp_w2_v, nsa_w_out, ffn_w_up, ffn_conv, ffn_conv_bias, ffn_w_down):
    depth = norm_mix.shape[0]
    h = x
    for layer in range(depth):
        e = layer // 2
        if layer % 2 == 0:
            h = hybrid_layer(h, norm_mix[layer], hy_w_in[e], gdn_conv[e], gdn_a_log[e], gdn_dt_bias[e], gdn_norm[e],
                             ssd_conv[e], ssd_conv_bias[e], ssd_a_log[e], ssd_dt_bias[e], ssd_d[e], ssd_norm[e],
                             hy_w_out[e])
        else:
            h = nsa_layer(h, norm_mix[layer], nsa_w_in[e], cmp_pos_k[e], cmp_w1_k[e], cmp_w2_k[e], cmp_pos_v[e],
                          cmp_w1_v[e], cmp_w2_v[e], nsa_w_out[e])
        h = conv_ffn(h, norm_ffn[layer], ffn_w_up[layer], ffn_conv[layer], ffn_conv_bias[layer], ffn_w_down[layer],
                     norm_final, layer == depth - 1)
    return h
```

```python
import functools
import math

import numpy as np
import jax
import jax.numpy as jnp
from jax import lax
from jax.experimental import pallas as pl
from jax.experimental.pallas import tpu as pltpu

F32 = jnp.float32
BF16 = jnp.bfloat16
HIGHEST = lax.Precision.HIGHEST

EPS = 1e-6
LANES = 128
VMEM_LIMIT = 56 * 1024 * 1024

GDN_HEADS = 4
GDN_HEAD_DIM = 128
GDN_WIDTH = GDN_HEADS * GDN_HEAD_DIM
CHUNK = 64
SHORT_CONV = 4
SSD_HEADS = 8
SSD_HEAD_DIM = 64
SSD_WIDTH = SSD_HEADS * SSD_HEAD_DIM
SSD_GROUPS = 2
SSD_STATE = 128
SSD_XBC = SSD_WIDTH + 2 * SSD_GROUPS * SSD_STATE
NSA_HEADS = 16
NSA_KV_GROUPS = 2
NSA_HEAD_DIM = 64
NSA_KV = NSA_KV_GROUPS * NSA_HEAD_DIM
NSA_J = NSA_HEADS // NSA_KV_GROUPS
CMP_BLOCK = 32
CMP_STRIDE = 16
CMP_HIDDEN = 256
SEL_BLOCK = 64
SEL_TOPK = 6
WINDOW = 512
SEL_FORCE = 1e9
NEG_BIG = -1e30
FFN_CONV = 3
FF_CHUNK = 256


def _cparams(*sem):
    return pltpu.CompilerParams(dimension_semantics=sem, vmem_limit_bytes=VMEM_LIMIT)


def _const_spec(shape):
    nd = len(shape)
    return pl.BlockSpec(shape, lambda *_: (0,) * nd, pipeline_mode=pl.Buffered(1))


def _rms(x, w):
    return x * lax.rsqrt(jnp.mean(x * x, axis=-1, keepdims=True) + EPS) * w


def _silu(x):
    return x * (1.0 / (1.0 + jnp.exp(-x)))


def _softplus(x):
    return jnp.maximum(x, 0.0) + jnp.log1p(jnp.exp(-jnp.abs(x)))


def _bdot(a, b):
    return jnp.dot(a.astype(BF16), b.astype(BF16), preferred_element_type=F32)


def _bdot_nt(a, b):
    return lax.dot_general(a.astype(BF16), b.astype(BF16), (((1,), (1,)), ((), ())), preferred_element_type=F32)


def _bdot_tn(a, b):
    return lax.dot_general(a.astype(BF16), b.astype(BF16), (((0,), (0,)), ((), ())), preferred_element_type=F32)


def _fdot(a, b):
    return jnp.dot(a, b, preferred_element_type=F32, precision=HIGHEST)


def _norm_proj_kernel(x_ref, nw_ref, *refs, transposed):
    n_out = len(transposed)
    xn = _rms(x_ref[...], nw_ref[...]).astype(BF16)
    for w_ref, o_ref, tr in zip(refs[:n_out], refs[n_out:], transposed):
        if tr:
            res = lax.dot_general(w_ref[...], xn, (((1,), (1,)), ((), ())), preferred_element_type=F32)
        else:
            res = jnp.dot(xn, w_ref[...], preferred_element_type=F32)
        o_ref[...] = res.astype(o_ref.dtype)


def norm_proj(h2, nw, weights, transposed=None, tm=512):
    T, D = h2.shape
    transposed = tuple(transposed or (False,) * len(weights))
    ws = [(w.T if tr else w).astype(BF16) for w, tr in zip(weights, transposed)]
    out_specs, out_shape = [], []
    for w, tr in zip(weights, transposed):
        n = w.shape[1]
        out_specs.append(pl.BlockSpec((n, tm), lambda i: (0, i)) if tr else pl.BlockSpec((tm, n), lambda i: (i, 0)))
        out_shape.append(jax.ShapeDtypeStruct((n, T) if tr else (T, n), F32))
    return pl.pallas_call(
        functools.partial(_norm_proj_kernel, transposed=transposed),
        grid=(T // tm,),
        in_specs=[pl.BlockSpec((tm, D), lambda i: (i, 0)), _const_spec((1, D))] + [_const_spec(w.shape) for w in ws],
        out_specs=out_specs,
        out_shape=out_shape,
        compiler_params=_cparams("parallel"),
        name="norm_proj",
    )(h2, nw.reshape(1, D), *ws)


def _proj_residual_kernel(h_ref, *refs, n_in, transposed):
    acc = h_ref[...]
    for a_ref, w_ref in zip(refs[:n_in], refs[n_in:2 * n_in]):
        if transposed:
            acc = acc + _bdot_tn(a_ref[...], w_ref[...])
        else:
            acc = acc + jnp.dot(a_ref[...].astype(BF16), w_ref[...], preferred_element_type=F32)
    refs[2 * n_in][...] = acc


def proj_residual(h2, acts, weights, transposed=False, tm=512):
    T, D = h2.shape
    n_in = len(acts)
    if transposed:
        act_specs = [pl.BlockSpec((a.shape[0], tm), lambda i: (0, i)) for a in acts]
    else:
        act_specs = [pl.BlockSpec((tm, a.shape[1]), lambda i: (i, 0)) for a in acts]
    return pl.pallas_call(
        functools.partial(_proj_residual_kernel, n_in=n_in, transposed=transposed),
        grid=(T // tm,),
        in_specs=[pl.BlockSpec((tm, D), lambda i: (i, 0))] + act_specs
        + [_const_spec(w.shape) for w in weights],
        out_specs=pl.BlockSpec((tm, D), lambda i: (i, 0)),
        out_shape=jax.ShapeDtypeStruct((T, D), F32),
        compiler_params=_cparams("parallel"),
        name="proj_residual",
    )(h2, *acts, *weights)


def _ffn_kernel(h_ref, nw_ref, wup_ref, cw_ref, cb_ref, wd_ref, fw_ref, o_ref,
                xn_ref, acc_ref, pg_ref, pu_ref, *, n_chunks, final_norm):
    tm = h_ref.shape[0]
    dff = wd_ref.shape[0]
    gate_cols = lambda c: slice(c * FF_CHUNK, (c + 1) * FF_CHUNK)
    up_cols = lambda c: slice(dff + c * FF_CHUNK, dff + (c + 1) * FF_CHUNK)

    @pl.when(pl.program_id(1) == 0)
    def _():
        pg_ref[...] = jnp.zeros_like(pg_ref)
        pu_ref[...] = jnp.zeros_like(pu_ref)

    x = h_ref[...]
    xn_ref[...] = _rms(x, nw_ref[...]).astype(BF16)
    acc_ref[...] = x
    row = lax.broadcasted_iota(jnp.int32, (tm, 1), 0)

    def conv(u, prev, w, b):
        m1 = jnp.where(row == 0, prev[7:8], pltpu.roll(u, 1, 0))
        m2 = jnp.where(row == 0, prev[6:7], jnp.where(row == 1, prev[7:8], pltpu.roll(u, 2, 0)))
        return u * w[2:3] + m1 * w[1:2] + m2 * w[0:1] + b

    def up(c):
        xn = xn_ref[...]
        return (jnp.dot(xn, wup_ref[:, gate_cols(c)], preferred_element_type=F32),
                jnp.dot(xn, wup_ref[:, up_cols(c)], preferred_element_type=F32))

    nxt = up(0)
    for c in range(n_chunks):
        ug, uu = nxt
        if c + 1 < n_chunks:
            nxt = up(c + 1)
        yg = conv(ug, pg_ref[c], cw_ref[:, gate_cols(c)], cb_ref[:, gate_cols(c)])
        yu = conv(uu, pu_ref[c], cw_ref[:, up_cols(c)], cb_ref[:, up_cols(c)])
        pg_ref[c] = ug[tm - 8:tm]
        pu_ref[c] = uu[tm - 8:tm]
        a = (_silu(yg) * yu).astype(BF16)
        acc_ref[...] += jnp.dot(a, wd_ref[gate_cols(c), :], preferred_element_type=F32)
    out = acc_ref[...]
    if final_norm:
        out = _rms(out, fw_ref[...])
    o_ref[...] = out


def conv_ffn(h3, nw, w_up, conv_w, conv_b, w_down, final_w, final_norm, tm=512):
    B, S, D = h3.shape
    dff = w_down.shape[0]
    nch = dff // FF_CHUNK
    consts = (w_up.astype(BF16), conv_w, conv_b.reshape(1, 2 * dff), w_down.astype(BF16))
    tile = pl.BlockSpec((None, tm, D), lambda b, s: (b, s, 0))
    return pl.pallas_call(
        functools.partial(_ffn_kernel, n_chunks=nch, final_norm=final_norm),
        grid=(B, S // tm),
        in_specs=[tile, _const_spec((1, D))] + [_const_spec(a.shape) for a in consts] + [_const_spec((1, D))],
        out_specs=tile,
        out_shape=jax.ShapeDtypeStruct((B, S, D), F32),
        scratch_shapes=[pltpu.VMEM((tm, D), BF16), pltpu.VMEM((tm, D), F32),
                        pltpu.VMEM((nch, 8, FF_CHUNK), F32), pltpu.VMEM((nch, 8, FF_CHUNK), F32)],
        compiler_params=_cparams("parallel", "arbitrary"),
        name="conv_ffn",
    )(h3, nw.reshape(1, D), *consts, final_w.reshape(1, D))


def _causal_conv(x, xbuf, w, ts, width):
    xbuf[8:8 + ts, :] = x
    y = x * w[width - 1:width]
    for k in range(1, width):
        y = y + xbuf[pl.ds(8 - k, ts), :] * w[width - 1 - k:width - k]
    xbuf[0:8, :] = x[ts - 8:ts]
    return y


def _gdn_kernel(qkv_ref, sm_ref, z_ref, cw_ref, alog_ref, dtb_ref, gn_ref, o_ref,
                xbuf, qn, kn, vn, gb, state, *, ts):
    nh, dk, C = GDN_HEADS, GDN_HEAD_DIM, CHUNK
    R = nh * C

    @pl.when(pl.program_id(1) == 0)
    def _():
        xbuf[0:8, :] = jnp.zeros((8, xbuf.shape[1]), F32)
        state[...] = jnp.zeros_like(state)

    y = _silu(_causal_conv(qkv_ref[...], xbuf, cw_ref[...], ts, SHORT_CONV))
    for h in range(nh):
        qh = y[:, h * dk:(h + 1) * dk]
        qn[h] = qh * (lax.rsqrt(jnp.sum(qh * qh, axis=-1, keepdims=True) + EPS) * dk ** -0.5)
        kh = y[:, GDN_WIDTH + h * dk:GDN_WIDTH + (h + 1) * dk]
        kn[h] = kh * lax.rsqrt(jnp.sum(kh * kh, axis=-1, keepdims=True) + EPS)
        vn[h] = y[:, 2 * GDN_WIDTH + h * dk:2 * GDN_WIDTH + (h + 1) * dk]
    sm = sm_ref[...]
    lane = lax.broadcasted_iota(jnp.int32, sm.shape, 1)
    beta = 1.0 / (1.0 + jnp.exp(-sm))
    g = -jnp.exp(alog_ref[...]) * _softplus(sm + dtb_ref[...])
    gb[...] = jnp.where(lane < nh, beta, g)

    ri = lax.broadcasted_iota(jnp.int32, (R, R), 0)
    ci = lax.broadcasted_iota(jnp.int32, (R, R), 1)
    same_head = (ri >> 6) == (ci >> 6)
    causal = same_head & (ri >= ci)
    strict = same_head & (ri > ci)
    eye = (ri == ci).astype(F32)
    l64 = (lax.broadcasted_iota(jnp.int32, (C, C), 0) >= lax.broadcasted_iota(jnp.int32, (C, C), 1)).astype(F32)
    gnw = gn_ref[...]

    def stack(ref, r0):
        return jnp.concatenate([ref[h, pl.ds(r0, C), :] for h in range(nh)], axis=0)

    chunks = range(ts // C)
    each = lambda fn: [fn(c) for c in chunks]
    gbc = each(lambda c: gb[c * C:(c + 1) * C, :])
    gam = each(lambda c: _fdot(l64, gbc[c]))
    q = each(lambda c: stack(qn, c * C))
    k = each(lambda c: stack(kn, c * C))
    v = each(lambda c: stack(vn, c * C))
    beta_col = each(lambda c: jnp.concatenate([gbc[c][:, h:h + 1] for h in range(nh)], axis=0))
    gam_col = each(lambda c: jnp.concatenate([gam[c][:, nh + h:nh + h + 1] for h in range(nh)], axis=0))
    decay = each(lambda c: jnp.exp(jnp.where(
        causal, gam_col[c] - jnp.sum(eye * gam_col[c], axis=0, keepdims=True), NEG_BIG)))
    kb = each(lambda c: k[c] * beta_col[c])
    p = each(lambda c: jnp.where(strict, -(_bdot_nt(kb[c], k[c]) * decay[c]), 0.0))
    t = each(lambda c: eye + p[c])
    for _ in range(5):
        p = each(lambda c: _bdot(p[c], p[c]))
        t = each(lambda c: t[c] + _bdot(t[c], p[c]))
    eg = each(lambda c: jnp.exp(gam_col[c]))
    sol = each(lambda c: _bdot(t[c], jnp.concatenate([v[c] * beta_col[c], kb[c] * eg[c]], axis=1)))
    qk = each(lambda c: _bdot_nt(q[c], k[c]) * decay[c])
    gl = each(lambda c: [gam[c][C - 1:C, nh + h:nh + h + 1] for h in range(nh)])
    k_dec = each(lambda c: k[c] * jnp.exp(
        jnp.concatenate([jnp.broadcast_to(gl[c][h], (C, 1)) for h in range(nh)], axis=0) - gam_col[c]))
    wq = each(lambda c: [jnp.concatenate([sol[c][h * C:(h + 1) * C, dk:], (q[c] * eg[c])[h * C:(h + 1) * C]], axis=0)
                         for h in range(nh)])

    heads = range(nh)
    s_cur = [state[h] for h in heads]
    for c in chunks:
        ws = [_bdot(wq[c][h], s_cur[h]) for h in heads]
        u = [sol[c][h * C:(h + 1) * C, :dk] - ws[h][:C] for h in heads]
        o = [ws[h][C:] + _bdot(qk[c][h * C:(h + 1) * C, h * C:(h + 1) * C], u[h]) for h in heads]
        s_cur = [s_cur[h] * jnp.exp(gl[c][h]) + _bdot_tn(k_dec[c][h * C:(h + 1) * C], u[h]) for h in heads]
        for h in heads:
            zh = z_ref[c * C:(c + 1) * C, h * dk:(h + 1) * dk]
            o_ref[c * C:(c + 1) * C, h * dk:(h + 1) * dk] = _rms(o[h], gnw) * _silu(zh)
    for h in heads:
        state[h] = s_cur[h]


def gdn_heads(qkv, small, z_a, conv_w, a_log, dt_bias, gnorm, ts=256):
    B, S, _ = qkv.shape
    nh, dk = GDN_HEADS, GDN_HEAD_DIM
    alog_l = jnp.zeros((1, LANES), F32).at[0, nh:2 * nh].set(a_log)
    dtb_l = jnp.zeros((1, LANES), F32).at[0, nh:2 * nh].set(dt_bias)

    def tile(n):
        return pl.BlockSpec((None, ts, n), lambda b, s: (b, s, 0))

    return pl.pallas_call(
        functools.partial(_gdn_kernel, ts=ts),
        grid=(B, S // ts),
        in_specs=[tile(3 * GDN_WIDTH), tile(LANES), tile(GDN_WIDTH), _const_spec(conv_w.shape),
                  _const_spec((1, LANES)), _const_spec((1, LANES)), _const_spec((1, dk))],
        out_specs=tile(GDN_WIDTH),
        out_shape=jax.ShapeDtypeStruct((B, S, GDN_WIDTH), F32),
        scratch_shapes=[pltpu.VMEM((ts + 8, 3 * GDN_WIDTH), F32)]
        + [pltpu.VMEM((nh, ts, dk), F32)] * 3
        + [pltpu.VMEM((ts, LANES), F32), pltpu.VMEM((nh, dk, dk), F32)],
        compiler_params=_cparams("parallel", "arbitrary"),
        name="gdn_heads",
    )(qkv, small, z_a, conv_w, alog_l, dtb_l, gnorm.reshape(1, dk))


def _ssd_kernel(xbc_ref, sm_ref, z_ref, cw_ref, cb_ref, dtb_ref, alog_ref, d_ref, nw_ref, e_ref, o_ref,
                xbuf, xs_s, b_s, c_s, dt_s, hstate, *, ts):
    C, P, N = CHUNK, SSD_HEAD_DIM, SSD_STATE
    hpg = SSD_HEADS // SSD_GROUPS
    gw = hpg * P

    @pl.when(pl.program_id(1) == 0)
    def _():
        xbuf[0:8, :] = jnp.zeros((8, xbuf.shape[1]), F32)
        hstate[...] = jnp.zeros_like(hstate)

    y = _silu(_causal_conv(xbc_ref[...], xbuf, cw_ref[...], ts, SHORT_CONV) + cb_ref[...])
    xs_s[...] = y[:, :SSD_WIDTH]
    b_s[...] = y[:, SSD_WIDTH:SSD_WIDTH + SSD_GROUPS * N]
    c_s[...] = y[:, SSD_WIDTH + SSD_GROUPS * N:]
    sm = sm_ref[...]
    lane = lax.broadcasted_iota(jnp.int32, sm.shape, 1)
    dt = jnp.where((lane >= SSD_HEADS) & (lane < 2 * SSD_HEADS), _softplus(sm + dtb_ref[...]), 0.0)
    dt_s[...] = _fdot(dt, e_ref[...])
    a_l = -jnp.exp(alog_ref[...])
    d_l = d_ref[...]
    nw = nw_ref[...]

    li = lax.broadcasted_iota(jnp.int32, (C, C), 0)
    si = lax.broadcasted_iota(jnp.int32, (C, C), 1)
    causal = li >= si
    l64 = causal.astype(F32)
    eye = (li == si).astype(F32)

    def chunk(c, carry):
        r0 = pl.multiple_of(c * C, C)
        dtc = dt_s[pl.ds(r0, C), :]
        xs = xs_s[pl.ds(r0, C), :]
        xc = xs * dtc
        acs = _fdot(l64, dtc * a_l)
        acs_last = acs[C - 1:C, :]
        eacs = jnp.exp(acs)
        xd = xc * jnp.exp(acs_last - acs)
        e_last = jnp.exp(acs_last)
        ys = []
        for g in range(SSD_GROUPS):
            gl = slice(g * gw, (g + 1) * gw)
            bg = b_s[pl.ds(r0, C), g * N:(g + 1) * N]
            cg = c_s[pl.ds(r0, C), g * N:(g + 1) * N]
            cb = _bdot_nt(cg, bg)
            h_in = hstate[g]
            y_off = _bdot(cg, h_in) * eacs[:, gl]
            yd = []
            for j in range(hpg):
                hl = slice((g * hpg + j) * P, (g * hpg + j + 1) * P)
                blk = acs[:, hl]
                row = jnp.sum(blk * eye, axis=0, keepdims=True)
                m = cb * jnp.exp(jnp.where(causal, blk - row, NEG_BIG))
                yd.append(_bdot(m, xc[:, hl]))
            hstate[g] = h_in * e_last[:, gl] + _bdot_tn(bg, xd[:, gl])
            ys.append(jnp.concatenate(yd, axis=1) + y_off)
        yy = jnp.concatenate(ys, axis=1) + d_l * xs
        yy = yy * _silu(z_ref[pl.ds(r0, C), :])
        o_ref[pl.ds(r0, C), :] = jnp.concatenate(
            [_rms(yy[:, g * gw:(g + 1) * gw], nw[:, g * gw:(g + 1) * gw]) for g in range(SSD_GROUPS)], axis=1)
        return carry

    lax.fori_loop(0, ts // C, chunk, 0, unroll=True)


def ssd_heads(xbc, small, z_b, conv_w, conv_b, a_log, dt_bias, d_skip, norm_w, ts=256):
    B, S, _ = xbc.shape
    nh, P = SSD_HEADS, SSD_HEAD_DIM
    dtb_l = jnp.zeros((1, LANES), F32).at[0, nh:2 * nh].set(dt_bias)
    expand = np.zeros((LANES, SSD_WIDTH), np.float32)
    for h in range(nh):
        expand[nh + h, h * P:(h + 1) * P] = 1.0

    def tile(n):
        return pl.BlockSpec((None, ts, n), lambda b, s: (b, s, 0))

    gw = SSD_WIDTH // SSD_GROUPS
    return pl.pallas_call(
        functools.partial(_ssd_kernel, ts=ts),
        grid=(B, S // ts),
        in_specs=[tile(SSD_XBC), tile(LANES), tile(SSD_WIDTH), _const_spec(conv_w.shape), _const_spec((1, SSD_XBC)),
                  _const_spec((1, LANES)), _const_spec((1, SSD_WIDTH)), _const_spec((1, SSD_WIDTH)),
                  _const_spec((1, SSD_WIDTH)), _const_spec((LANES, SSD_WIDTH))],
        out_specs=tile(SSD_WIDTH),
        out_shape=jax.ShapeDtypeStruct((B, S, SSD_WIDTH), F32),
        scratch_shapes=[pltpu.VMEM((ts + 8, SSD_XBC), F32), pltpu.VMEM((ts, SSD_WIDTH), F32),
                        pltpu.VMEM((ts, SSD_GROUPS * SSD_STATE), F32), pltpu.VMEM((ts, SSD_GROUPS * SSD_STATE), F32),
                        pltpu.VMEM((ts, SSD_WIDTH), F32), pltpu.VMEM((SSD_GROUPS, SSD_STATE, gw), F32)],
        compiler_params=_cparams("parallel", "arbitrary"),
        name="ssd_heads",
    )(xbc, small, z_b, conv_w, conv_b.reshape(1, SSD_XBC), dtb_l, jnp.repeat(a_log, P).reshape(1, SSD_WIDTH),
      jnp.repeat(d_skip, P).reshape(1, SSD_WIDTH), norm_w.reshape(1, SSD_WIDTH), jnp.asarray(expand))


def hybrid_layer(h, nw, w_in, gdn_conv, gdn_a_log, gdn_dt_bias, gdn_norm, ssd_conv, ssd_conv_bias, ssd_a_log,
                 ssd_dt_bias, ssd_d, ssd_norm, w_out):
    B, S, D = h.shape
    T = B * S
    gwid, swid = GDN_WIDTH, SSD_WIDTH
    o_za = 3 * gwid
    o_ba = o_za + gwid
    o_zb = o_ba + 2 * GDN_HEADS
    o_xbc = o_zb + swid
    o_dt = o_xbc + SSD_XBC
    w_small = jnp.zeros((D, LANES), F32)
    w_small = w_small.at[:, :2 * GDN_HEADS].set(w_in[:, o_ba:o_zb])
    w_small = w_small.at[:, SSD_HEADS:2 * SSD_HEADS].set(w_in[:, o_dt:o_dt + SSD_HEADS])
    ws = [w_in[:, :o_za], w_in[:, o_za:o_ba], w_small, w_in[:, o_zb:o_xbc], w_in[:, o_xbc:o_dt]]
    h2 = h.reshape(T, D)
    qkv, z_a, small, z_b, xbc = norm_proj(h2, nw, [w.astype(BF16) for w in ws])
    small3 = small.reshape(B, S, LANES)
    o_a = gdn_heads(qkv.reshape(B, S, -1), small3, z_a.reshape(B, S, -1), gdn_conv, gdn_a_log, gdn_dt_bias, gdn_norm)
    y_b = ssd_heads(xbc.reshape(B, S, -1), small3, z_b.reshape(B, S, -1), ssd_conv, ssd_conv_bias, ssd_a_log,
                    ssd_dt_bias, ssd_d, ssd_norm)
    out = proj_residual(h2, [o_a.reshape(T, gwid), y_b.reshape(T, swid)],
                        [w_out[:gwid].astype(BF16), w_out[gwid:].astype(BF16)])
    return out.reshape(B, S, D)


def _compress_kernel(kc0_ref, kc1_ref, vc0_ref, vc1_ref, pk0_ref, pk1_ref, pv0_ref, pv1_ref,
                     w1ka_ref, w1kb_ref, w1va_ref, w1vb_ref, w2k_ref, w2v_ref, ko_ref, vo_ref):
    n_seg = kc0_ref.shape[0]
    for srcs, p0, p1, wa, wb, w2, out in (((kc0_ref, kc1_ref), pk0_ref, pk1_ref, w1ka_ref, w1kb_ref, w2k_ref, ko_ref),
                                          ((vc0_ref, vc1_ref), pv0_ref, pv1_ref, w1va_ref, w1vb_ref, w2v_ref, vo_ref)):
        for g in range(NSA_KV_GROUPS):
            t = srcs[g][...]
            f0 = _bdot(t + p0[...], wa[...])
            f1 = _bdot(t + p1[...], wb[...])
            hid = f0 + pltpu.roll(f1, n_seg - 1, 0)
            out[g] = _bdot(_silu(hid), w2[...])


def nsa_compress(kc_groups, vc_groups, pos_k, w1_k, w2_k, pos_v, w1_v, w2_v):
    B, n_seg, seg_w = kc_groups[0].shape
    G = NSA_KV_GROUPS
    half = CMP_STRIDE
    args = [*kc_groups, *vc_groups,
            pos_k[:half].reshape(1, seg_w), pos_k[half:].reshape(1, seg_w),
            pos_v[:half].reshape(1, seg_w), pos_v[half:].reshape(1, seg_w),
            w1_k[:seg_w].astype(BF16), w1_k[seg_w:].astype(BF16), w1_v[:seg_w].astype(BF16), w1_v[seg_w:].astype(BF16),
            w2_k.astype(BF16), w2_v.astype(BF16)]
    per_b = pl.BlockSpec((None, n_seg, seg_w), lambda b: (b, 0, 0))
    out_b = pl.BlockSpec((None, G, n_seg, NSA_HEAD_DIM), lambda b: (b, 0, 0, 0))
    return pl.pallas_call(
        _compress_kernel,
        grid=(B,),
        in_specs=[per_b] * (2 * G) + [_const_spec(a.shape) for a in args[2 * G:]],
        out_specs=[out_b, out_b],
        out_shape=[jax.ShapeDtypeStruct((B, G, n_seg, NSA_HEAD_DIM), F32)] * 2,
        compiler_params=_cparams("parallel"),
        name="nsa_compress",
    )(*args)


def _alibi_slopes():
    return [[float(2.0 ** (-8.0 * (g * NSA_J + j + 1) / NSA_HEADS)) for j in range(NSA_J)]
            for g in range(NSA_KV_GROUPS)]


M_INIT = -1e20


def _softmax_step(carry, s, pen, v):
    m, l, acc = carry
    s = s - pen
    m_new = jnp.maximum(m, jnp.max(s, axis=-1, keepdims=True))
    alpha = jnp.exp(m - m_new)
    e = jnp.exp(s - m_new)
    l = alpha * l + jnp.sum(e, axis=-1, keepdims=True)
    acc = alpha * acc + _bdot(e, v)
    return m_new, l, acc


def _nsa_kernel(q_ref, gt_ref, kc_ref, vc_ref, ks_ref, vs_ref, kw_ref, vw_ref, cov_ref, o_ref,
                *, tq, tk, seq, n_cmp, n_sel):
    G, J, Dh = NSA_KV_GROUPS, NSA_J, NSA_HEAD_DIM
    slopes = _alibi_slopes()
    start = pl.program_id(1) * tq
    t_col = start + lax.broadcasted_iota(jnp.int32, (tq, 1), 0)
    gates = 1.0 / (1.0 + jnp.exp(-gt_ref[...]))
    nc = kc_ref.shape[1]
    rows = J * tq
    top_k = min(SEL_TOPK, n_sel)

    def init():
        return (jnp.full((rows, 1), M_INIT, F32), jnp.zeros((rows, 1), F32), jnp.zeros((rows, Dh), F32))

    def penalty(dist, valid, g):
        distf = dist.astype(F32)
        mask_pen = jnp.where(valid, 0.0, -NEG_BIG)
        return jnp.concatenate([slopes[g][j] * distf + mask_pen for j in range(J)], axis=0)

    def finish(carry):
        _, l, acc = carry
        return acc / jnp.maximum(l, 1e-30)

    for g in range(G):
        qg = jnp.concatenate([q_ref[:, (g * J + j) * Dh:(g * J + j + 1) * Dh] for j in range(J)], axis=0) * Dh ** -0.5

        n_idx = lax.broadcasted_iota(jnp.int32, (tq, nc), 1)
        dist_c = t_col - (n_idx * CMP_STRIDE + (CMP_BLOCK - 1))
        valid_c = (dist_c >= 0) & (n_idx < n_cmp)
        s_c = lax.dot_general(qg, kc_ref[g], (((1,), (1,)), ((), ())), preferred_element_type=F32, precision=HIGHEST)
        s_c = s_c - penalty(dist_c, valid_c, g)
        m_c = jnp.maximum(jnp.max(s_c, axis=-1, keepdims=True), M_INIT)
        e_c = jnp.exp(s_c - m_c)
        p_c = e_c / jnp.maximum(jnp.sum(e_c, axis=-1, keepdims=True), 1e-30)
        o_c = _bdot(p_c, vc_ref[g])
        p_sum = p_c[0:tq]
        for j in range(1, J):
            p_sum = p_sum + p_c[j * tq:(j + 1) * tq]
        imp = _fdot(p_sum, cov_ref[...])

        blk = lax.broadcasted_iota(jnp.int32, (tq, LANES), 1)
        cur = t_col >> 6
        forced = (blk == 0) | (blk == cur) | (blk == cur - 1)
        imp = jnp.where(forced, SEL_FORCE, jnp.where(blk * SEL_BLOCK <= t_col, imp, -SEL_FORCE))
        imp = jnp.where(blk < n_sel, imp, -3e38)
        sel = jnp.zeros((tq, LANES), F32)
        for _ in range(top_k):
            mx = jnp.max(imp, axis=-1, keepdims=True)
            first = jnp.min(jnp.where(imp == mx, blk, LANES), axis=-1, keepdims=True)
            hit = blk == first
            sel = jnp.where(hit, 1.0, sel)
            imp = jnp.where(hit, -jnp.inf, imp)

        def sel_step(kt, carry):
            k0 = pl.multiple_of(kt * tk, tk)
            pos = k0 + lax.broadcasted_iota(jnp.int32, (tq, tk), 1)
            e_blk = lax.broadcasted_iota(jnp.int32, (LANES, tk), 0)
            e_pos = k0 + lax.broadcasted_iota(jnp.int32, (LANES, tk), 1)
            expand = ((e_pos >> 6) == e_blk).astype(BF16)
            chosen = jnp.dot(sel.astype(BF16), expand, preferred_element_type=F32) > 0.5
            dist = t_col - pos
            s = _bdot_nt(qg, ks_ref[pl.ds(k0, tk), g * Dh:(g + 1) * Dh])
            return _softmax_step(carry, s, penalty(dist, chosen & (dist >= 0), g),
                                 vs_ref[pl.ds(k0, tk), g * Dh:(g + 1) * Dh])

        n_kt = (start + tq + tk - 1) // tk
        o_s = finish(lax.fori_loop(0, n_kt, sel_step, init()))

        band0 = jnp.clip(start - WINDOW, 0, seq - (WINDOW + tq))

        def win_step(kt, carry):
            k0 = pl.multiple_of(band0 + kt * tq, tq)
            pos = k0 + lax.broadcasted_iota(jnp.int32, (tq, tq), 1)
            dist = t_col - pos
            s = _bdot_nt(qg, kw_ref[pl.ds(k0, tq), g * Dh:(g + 1) * Dh])
            return _softmax_step(carry, s, penalty(dist, (dist >= 0) & (dist < WINDOW), g),
                                 vw_ref[pl.ds(k0, tq), g * Dh:(g + 1) * Dh])

        o_w = finish(lax.fori_loop(0, (WINDOW + tq) // tq, win_step, init()))

        outs = []
        for j in range(J):
            r = slice(j * tq, (j + 1) * tq)
            col = g * J + j
            outs.append(gates[:, col:col + 1] * o_c[r] + gates[:, NSA_HEADS + col:NSA_HEADS + col + 1] * o_s[r]
                        + gates[:, 2 * NSA_HEADS + col:2 * NSA_HEADS + col + 1] * o_w[r])
        o_ref[:, g * J * Dh:(g + 1) * J * Dh] = jnp.concatenate(outs, axis=1)


def nsa_attention(q, gates, k_cmp, v_cmp, k_s, v_s, k_w, v_w, tq=128, tk=256):
    B, S, _ = q.shape
    n_seg = k_cmp.shape[2]
    n_cmp = (S - CMP_BLOCK) // CMP_STRIDE + 1
    n_sel = S // SEL_BLOCK
    cs = np.arange(n_seg) * CMP_STRIDE
    ss = np.arange(n_sel) * SEL_BLOCK
    cover = np.clip(np.minimum(cs[:, None] + CMP_BLOCK, ss[None, :] + SEL_BLOCK)
                    - np.maximum(cs[:, None], ss[None, :]), 0, None) / CMP_STRIDE
    cover_pad = np.zeros((n_seg, LANES), np.float32)
    cover_pad[:n_cmp, :n_sel] = cover[:n_cmp]
    width = NSA_HEADS * NSA_HEAD_DIM

    def tile(n):
        return pl.BlockSpec((None, tq, n), lambda b, s: (b, s, 0))

    per_b_cmp = pl.BlockSpec((None, NSA_KV_GROUPS, n_seg, NSA_HEAD_DIM), lambda b, s: (b, 0, 0, 0))
    per_b_seq = pl.BlockSpec((None, S, NSA_KV), lambda b, s: (b, 0, 0))
    return pl.pallas_call(
        functools.partial(_nsa_kernel, tq=tq, tk=tk, seq=S, n_cmp=n_cmp, n_sel=n_sel),
        grid=(B, S // tq),
        in_specs=[tile(width), tile(LANES), per_b_cmp, per_b_cmp, per_b_seq, per_b_seq, per_b_seq, per_b_seq,
                  _const_spec((n_seg, LANES))],
        out_specs=tile(width),
        out_shape=jax.ShapeDtypeStruct((B, S, width), F32),
        compiler_params=_cparams("parallel", "arbitrary"),
        name="nsa_attention",
    )(q, gates, k_cmp, v_cmp, k_s, v_s, k_w, v_w, jnp.asarray(cover_pad))


TQ = LANES
BF16_SUBLANES = 16
AUG = 128
POS_ROW0 = NSA_HEAD_DIM
SEL_ROW0 = POS_ROW0 + BF16_SUBLANES
ZERO_ROWS = BF16_SUBLANES
V_ROWS = NSA_HEAD_DIM + BF16_SUBLANES
POS_BITS = 6
N_SPLIT = 5
SEL_SHIFT = SEL_BLOCK.bit_length() - 1
LOG2E = math.log2(math.e)


def _slope_rows():
    G, J = NSA_KV_GROUPS, NSA_J
    rows = np.zeros((G, SEL_ROW0 - POS_ROW0, J * TQ), np.float32)
    for g, per_group in enumerate(_alibi_slopes()):
        for j, slope in enumerate(per_group):
            rest = float(np.float32(slope)) * LOG2E
            for i in range(N_SPLIT):
                piece = float(np.asarray(rest, dtype=BF16))
                rows[g, i, j * TQ:(j + 1) * TQ] = piece * 2.0 ** POS_BITS
                rows[g, N_SPLIT + i, j * TQ:(j + 1) * TQ] = piece
                rest -= piece
    return jnp.asarray(rows, BF16)


def _nsa_t_kernel(qT_ref, gT_ref, kc_ref, vc_ref, ks_ref, vsT_ref, kw_ref, vwT_ref, covT_ref, slope_ref, o_ref,
                  kaug_s, kaug_w, vaug_s, vaug_w, qaug, oc_ref, m_ref, acc_ref, *, tk, seq, n_cmp, n_sel):
    G, J, Dh, tq = NSA_KV_GROUPS, NSA_J, NSA_HEAD_DIM, TQ
    slopes = _alibi_slopes()
    qi = pl.program_id(1)
    start = qi * tq
    nc = kc_ref.shape[1]
    nsp = AUG - SEL_ROW0 - ZERO_ROWS
    top_k = min(SEL_TOPK, n_sel)
    lanes = J * tq

    @pl.when(qi == 0)
    def _():
        pos = lax.broadcasted_iota(jnp.int32, (seq, AUG - Dh), 0)
        col = lax.broadcasted_iota(jnp.int32, (seq, AUG - Dh), 1) + Dh
        pos_cols = jnp.where(col < POS_ROW0 + N_SPLIT, pos >> POS_BITS,
                             jnp.where(col < POS_ROW0 + 2 * N_SPLIT, pos & (2 ** POS_BITS - 1), 0)).astype(F32)
        onehot = ((col >= SEL_ROW0) & ((pos >> SEL_SHIFT) == col - SEL_ROW0)).astype(F32)
        ones_row = (lax.broadcasted_iota(jnp.int32, (V_ROWS - Dh, seq), 0) == 0).astype(F32)
        for g in range(G):
            for k_ref, ka, vT_ref, va, extra in ((ks_ref, kaug_s, vsT_ref, vaug_s, pos_cols + onehot),
                                                 (kw_ref, kaug_w, vwT_ref, vaug_w, pos_cols)):
                ka[g] = jnp.concatenate([k_ref[:, g * Dh:(g + 1) * Dh], extra], axis=1).astype(BF16)
                va[g] = jnp.concatenate([vT_ref[g * Dh:(g + 1) * Dh, :], ones_row], axis=0).astype(BF16)

    t_row = start + lax.broadcasted_iota(jnp.int32, (1, tq), 1)
    gates = 1.0 / (1.0 + jnp.exp(-gT_ref[...]))

    def per_head(x, fn):
        return jnp.concatenate([fn(j, x[:, j * tq:(j + 1) * tq]) for j in range(J)], axis=1)

    for g in range(G):
        qT = jnp.concatenate([qT_ref[(g * J + j) * Dh:(g * J + j + 1) * Dh, :] for j in range(J)], axis=1) * Dh ** -0.5

        n_idx = lax.broadcasted_iota(jnp.int32, (nc, tq), 0)
        dist_c = t_row - (n_idx * CMP_STRIDE + (CMP_BLOCK - 1))
        pen_c = jnp.where((dist_c >= 0) & (n_idx < n_cmp), 0.0, -NEG_BIG)
        distf_c = dist_c.astype(F32)
        s_c = per_head(_bdot(kc_ref[g], qT), lambda j, sj: sj - (slopes[g][j] * distf_c + pen_c))
        m_c = jnp.maximum(jnp.max(s_c, axis=0, keepdims=True), M_INIT)
        e_c = jnp.exp(s_c - m_c)
        p_c = e_c / jnp.maximum(jnp.sum(e_c, axis=0, keepdims=True), 1e-30)
        oc_ref[g] = _bdot_tn(vc_ref[g], p_c)
        p_sum = p_c[:, 0:tq]
        for j in range(1, J):
            p_sum = p_sum + p_c[:, j * tq:(j + 1) * tq]
        imp = _fdot(covT_ref[...], p_sum)

        blk = lax.broadcasted_iota(jnp.int32, (nsp, tq), 0)
        cur = t_row >> SEL_SHIFT
        forced = (blk == 0) | (blk == cur) | (blk == cur - 1)
        imp = jnp.where(forced, SEL_FORCE, jnp.where(blk * SEL_BLOCK <= t_row, imp, -SEL_FORCE))
        imp = jnp.where(blk < n_sel, imp, -3e38)
        sel_pen = jnp.full((nsp, tq), NEG_BIG, F32)
        for _ in range(top_k):
            mx = jnp.max(imp, axis=0, keepdims=True)
            first = jnp.min(jnp.where(imp == mx, blk, nsp), axis=0, keepdims=True)
            hit = blk == first
            sel_pen = jnp.where(hit, 0.0, sel_pen)
            imp = jnp.where(hit, -jnp.inf, imp)

        qaug[g] = jnp.concatenate(
            [(qT * LOG2E).astype(BF16), slope_ref[g], jnp.concatenate([sel_pen] * J, axis=1).astype(BF16),
             jnp.zeros((ZERO_ROWS, lanes), BF16)], axis=0)

    def attend(tile, k0_of, kaug, vaug):
        m_ref[...] = jnp.full(m_ref.shape, M_INIT, F32)
        acc_ref[...] = jnp.zeros(acc_ref.shape, F32)

        def scores(kt):
            k0 = pl.multiple_of(k0_of(kt), tile)
            return [jnp.dot(kaug[g, pl.ds(k0, tile), :], qaug[g], preferred_element_type=F32) for g in range(G)]

        def update(kt, sc, valid_fn):
            k0 = pl.multiple_of(k0_of(kt), tile)
            if valid_fn is not None:
                pos = k0 + lax.broadcasted_iota(jnp.int32, (tile, tq), 0)
                mask_pen = jnp.where(valid_fn(t_row - pos), 0.0, NEG_BIG)
            for g in range(G):
                s = sc[g]
                if valid_fn is not None:
                    s = per_head(s, lambda j, sj: sj + mask_pen)
                m_old = m_ref[g]
                m_new = jnp.maximum(m_old, jnp.max(s, axis=0, keepdims=True))
                e = jnp.exp2(s - m_new).astype(BF16)
                pv = jnp.dot(vaug[g, :, pl.ds(k0, tile)], e, preferred_element_type=F32)
                acc_ref[g] = jnp.exp2(m_old - m_new) * acc_ref[g] + pv
                m_ref[g] = m_new

        return scores, update

    def finish(g):
        acc = acc_ref[g]
        return acc[0:Dh] / jnp.maximum(acc[Dh:Dh + 1], 1e-30)

    n_past = (start + tq + tk - 1) // tk - 1
    sel_scores, sel_update = attend(tk, lambda kt: kt * tk, kaug_s, vaug_s)
    diag_valid = lambda dist: dist >= 0

    def past_pair(p, carry):
        s0 = sel_scores(2 * p)
        s1 = sel_scores(2 * p + 1)
        sel_update(2 * p, s0, None)
        sel_update(2 * p + 1, s1, None)
        return carry

    lax.fori_loop(0, n_past // 2, past_pair, 0)

    @pl.when(n_past % 2 == 1)
    def _():
        s0 = sel_scores(n_past - 1)
        s1 = sel_scores(n_past)
        sel_update(n_past - 1, s0, None)
        sel_update(n_past, s1, diag_valid)

    @pl.when(n_past % 2 == 0)
    def _():
        sel_update(n_past, sel_scores(n_past), diag_valid)

    o_s = [finish(g) for g in range(G)]

    band0 = jnp.clip(start - WINDOW, 0, seq - (WINDOW + tq))
    win_scores, win_update = attend(tq, lambda kt: band0 + kt * tq, kaug_w, vaug_w)
    n_win = (WINDOW + tq) // tq
    sc_next = win_scores(0)
    for kt in range(n_win):
        sc = sc_next
        if kt + 1 < n_win:
            sc_next = win_scores(kt + 1)
        win_update(kt, sc, lambda dist: (dist >= 0) & (dist < WINDOW))
    o_w = [finish(g) for g in range(G)]

    for g in range(G):
        def gate_row(branch):
            return jnp.concatenate([gates[branch * NSA_HEADS + g * J + j:branch * NSA_HEADS + g * J + j + 1, :]
                                    for j in range(J)], axis=1)

        out = gate_row(0) * oc_ref[g] + gate_row(1) * o_s[g] + gate_row(2) * o_w[g]
        for j in range(J):
            o_ref[(g * J + j) * Dh:(g * J + j + 1) * Dh, :] = out[:, j * tq:(j + 1) * tq].astype(o_ref.dtype)


def _nsa_t_kernel_old(qT_ref, gT_ref, kc_ref, vc_ref, ks_ref, vsT_ref, kw_ref, vwT_ref, covT_ref, slope_ref, o_ref,
                  kaug_s, kaug_w, vaug_s, vaug_w, selT, m_ref, acc_ref, *, tk, seq, n_cmp, n_sel):
    G, J, Dh, tq = NSA_KV_GROUPS, NSA_J, NSA_HEAD_DIM, TQ
    slopes = _alibi_slopes()
    qi = pl.program_id(1)
    start = qi * tq
    nc = kc_ref.shape[1]
    nsp = selT.shape[0]
    top_k = min(SEL_TOPK, n_sel)

    @pl.when(qi == 0)
    def _():
        pos = lax.broadcasted_iota(jnp.int32, (seq, AUG - Dh), 0)
        col = lax.broadcasted_iota(jnp.int32, (seq, AUG - Dh), 1)
        extra = jnp.where(col < N_SPLIT, pos >> POS_BITS,
                          jnp.where(col < 2 * N_SPLIT, pos & (2 ** POS_BITS - 1), 0)).astype(F32)
        ones_row = (lax.broadcasted_iota(jnp.int32, (V_ROWS - Dh, seq), 0) == 0).astype(F32)
        for g in range(G):
            for k_ref, ka, vT_ref, va in ((ks_ref, kaug_s, vsT_ref, vaug_s), (kw_ref, kaug_w, vwT_ref, vaug_w)):
                ka[g] = jnp.concatenate([k_ref[:, g * Dh:(g + 1) * Dh], extra], axis=1).astype(BF16)
                va[g] = jnp.concatenate([vT_ref[g * Dh:(g + 1) * Dh, :], ones_row], axis=0).astype(BF16)

    t_row = start + lax.broadcasted_iota(jnp.int32, (1, tq), 1)
    gates = 1.0 / (1.0 + jnp.exp(-gT_ref[...]))

    def per_head(x, fn):
        return jnp.concatenate([fn(j, x[:, j * tq:(j + 1) * tq]) for j in range(J)], axis=1)

    def attend(n_tiles, tile, k0_of, kaug, vaug, g, qaug, valid_fn):
        m_ref[...] = jnp.full(m_ref.shape, M_INIT, F32)
        acc_ref[...] = jnp.zeros(acc_ref.shape, F32)

        def step(kt, carry):
            k0 = pl.multiple_of(k0_of(kt), tile)
            pos = k0 + lax.broadcasted_iota(jnp.int32, (tile, tq), 0)
            mask_pen = jnp.where(valid_fn(kt, t_row - pos), 0.0, NEG_BIG)
            s = jnp.dot(kaug[g, pl.ds(k0, tile), :], qaug, preferred_element_type=F32)
            s = per_head(s, lambda j, sj: sj + mask_pen)
            m_old = m_ref[...]
            m_new = jnp.maximum(m_old, jnp.max(s, axis=0, keepdims=True))
            e = jnp.exp(s - m_new).astype(BF16)
            pv = jnp.dot(vaug[g, :, pl.ds(k0, tile)], e, preferred_element_type=F32)
            acc_ref[...] = jnp.exp(m_old - m_new) * acc_ref[...] + pv
            m_ref[...] = m_new
            return carry

        if isinstance(n_tiles, int):
            lax.fori_loop(0, n_tiles, step, 0, unroll=True)
        else:
            def pair(p, carry):
                return step(2 * p + 1, step(2 * p, carry))

            lax.fori_loop(0, n_tiles // 2, pair, 0)

            @pl.when(n_tiles % 2 == 1)
            def _():
                step(n_tiles - 1, 0)

        acc = acc_ref[...]
        return acc[0:Dh] / jnp.maximum(acc[Dh:Dh + 1], 1e-30)

    for g in range(G):
        qT = jnp.concatenate([qT_ref[(g * J + j) * Dh:(g * J + j + 1) * Dh, :] for j in range(J)], axis=1) * Dh ** -0.5

        n_idx = lax.broadcasted_iota(jnp.int32, (nc, tq), 0)
        dist_c = t_row - (n_idx * CMP_STRIDE + (CMP_BLOCK - 1))
        pen_c = jnp.where((dist_c >= 0) & (n_idx < n_cmp), 0.0, -NEG_BIG)
        distf_c = dist_c.astype(F32)
        s_c = per_head(_fdot(kc_ref[g], qT), lambda j, sj: sj - (slopes[g][j] * distf_c + pen_c))
        m_c = jnp.maximum(jnp.max(s_c, axis=0, keepdims=True), M_INIT)
        e_c = jnp.exp(s_c - m_c)
        p_c = e_c / jnp.maximum(jnp.sum(e_c, axis=0, keepdims=True), 1e-30)
        o_c = _bdot_tn(vc_ref[g], p_c)
        p_sum = p_c[:, 0:tq]
        for j in range(1, J):
            p_sum = p_sum + p_c[:, j * tq:(j + 1) * tq]
        imp = _fdot(covT_ref[...], p_sum)

        blk = lax.broadcasted_iota(jnp.int32, (nsp, tq), 0)
        cur = t_row >> 6
        forced = (blk == 0) | (blk == cur) | (blk == cur - 1)
        imp = jnp.where(forced, SEL_FORCE, jnp.where(blk * SEL_BLOCK <= t_row, imp, -SEL_FORCE))
        imp = jnp.where(blk < n_sel, imp, -3e38)
        sel = jnp.zeros((nsp, tq), F32)
        for _ in range(top_k):
            mx = jnp.max(imp, axis=0, keepdims=True)
            first = jnp.min(jnp.where(imp == mx, blk, nsp), axis=0, keepdims=True)
            hit = blk == first
            sel = jnp.where(hit, 1.0, sel)
            imp = jnp.where(hit, -jnp.inf, imp)
        selT[...] = sel

        qaug = jnp.concatenate([qT.astype(BF16), slope_ref[g]], axis=0)

        bpt = tk // SEL_BLOCK

        def sel_valid(kt, dist):
            chosen = jnp.concatenate(
                [jnp.broadcast_to(selT[pl.ds(kt * bpt + i, 1), :], (SEL_BLOCK, tq)) for i in range(bpt)], axis=0)
            return (chosen > 0.5) & (dist >= 0)

        o_s = attend((start + tq + tk - 1) // tk, tk, lambda kt: kt * tk, kaug_s, vaug_s, g, qaug, sel_valid)

        band0 = jnp.clip(start - WINDOW, 0, seq - (WINDOW + tq))
        o_w = attend((WINDOW + tq) // tq, tq, lambda kt: band0 + kt * tq, kaug_w, vaug_w, g, qaug,
                     lambda kt, dist: (dist >= 0) & (dist < WINDOW))

        def gate_row(branch):
            return jnp.concatenate([gates[branch * NSA_HEADS + g * J + j:branch * NSA_HEADS + g * J + j + 1, :]
                                    for j in range(J)], axis=1)

        out = gate_row(0) * o_c + gate_row(1) * o_s + gate_row(2) * o_w
        for j in range(J):
            o_ref[(g * J + j) * Dh:(g * J + j + 1) * Dh, :] = out[:, j * tq:(j + 1) * tq].astype(o_ref.dtype)


def nsa_attention_t(qT, gT, k_cmp, v_cmp, k_s, vT_s, k_w, vT_w, B, S, tk=256):
    width, T = qT.shape
    tq = TQ
    n_seg = k_cmp.shape[2]
    n_cmp = (S - CMP_BLOCK) // CMP_STRIDE + 1
    n_sel = S // SEL_BLOCK
    nsp = AUG - SEL_ROW0 - ZERO_ROWS
    assert n_sel <= nsp and S >= WINDOW + tq and POS_ROW0 + 2 * N_SPLIT <= SEL_ROW0
    cs = np.arange(n_seg) * CMP_STRIDE
    ss = np.arange(n_sel) * SEL_BLOCK
    cover = np.clip(np.minimum(cs[:, None] + CMP_BLOCK, ss[None, :] + SEL_BLOCK)
                    - np.maximum(cs[:, None], ss[None, :]), 0, None) / CMP_STRIDE
    cover_t = np.zeros((nsp, n_seg), np.float32)
    cover_t[:n_sel, :n_cmp] = cover[:n_cmp].T
    slope_rows = _slope_rows()
    nq = S // tq

    def qtile(n):
        return pl.BlockSpec((n, tq), lambda b, s: (0, b * nq + s))

    per_b_cmp = pl.BlockSpec((None, NSA_KV_GROUPS, n_seg, NSA_HEAD_DIM), lambda b, s: (b, 0, 0, 0))
    per_b_k = pl.BlockSpec((S, NSA_KV), lambda b, s: (b, 0))
    per_b_vT = pl.BlockSpec((NSA_KV, S), lambda b, s: (0, b))
    G = NSA_KV_GROUPS
    return pl.pallas_call(
        functools.partial(_nsa_t_kernel, tk=tk, seq=S, n_cmp=n_cmp, n_sel=n_sel),
        grid=(B, nq),
        in_specs=[qtile(width), qtile(LANES), per_b_cmp, per_b_cmp, per_b_k, per_b_vT, per_b_k, per_b_vT,
                  _const_spec(cover_t.shape), _const_spec(slope_rows.shape)],
        out_specs=qtile(width),
        out_shape=jax.ShapeDtypeStruct((width, T), BF16),
        scratch_shapes=[pltpu.VMEM((G, S, AUG), BF16), pltpu.VMEM((G, S, AUG), BF16),
                        pltpu.VMEM((G, V_ROWS, S), BF16), pltpu.VMEM((G, V_ROWS, S), BF16),
                        pltpu.VMEM((G, AUG, NSA_J * tq), BF16), pltpu.VMEM((G, NSA_HEAD_DIM, NSA_J * tq), F32),
                        pltpu.VMEM((G, 1, NSA_J * tq), F32), pltpu.VMEM((G, V_ROWS, NSA_J * tq), F32)],
        compiler_params=_cparams("parallel", "arbitrary"),
        name="nsa_attention",
    )(qT, gT, k_cmp, v_cmp, k_s, vT_s, k_w, vT_w, jnp.asarray(cover_t), slope_rows)


def nsa_layer(h, nw, w_in, cmp_pos_k, cmp_w1_k, cmp_w2_k, cmp_pos_v, cmp_w1_v, cmp_w2_v, w_out):
    B, S, D = h.shape
    T = B * S
    width = NSA_HEADS * NSA_HEAD_DIM
    G, Dh = NSA_KV_GROUPS, NSA_HEAD_DIM
    w_gates = jnp.zeros((D, LANES), F32).at[:, :3 * NSA_HEADS].set(w_in[:, width + 6 * NSA_KV:])
    kv = [w_in[:, width + i * NSA_KV:width + (i + 1) * NSA_KV] for i in range(6)]
    per_group = lambda w: [w[:, g * Dh:(g + 1) * Dh] for g in range(G)]
    ws = [w_in[:, :width]] + per_group(kv[0]) + per_group(kv[1]) + kv[2:] + [w_gates]
    h2 = h.reshape(T, D)
    outs = norm_proj(h2, nw, ws, transposed=(True,) + (False,) * (2 * G) + (False, True, False, True, True))
    qT, (k_s, vT_s, k_w, vT_w, gT) = outs[0], outs[1 + 2 * G:]
    n_seg = S // CMP_STRIDE
    segments = lambda t: t.reshape(B, n_seg, CMP_STRIDE * Dh)
    k_cmp, v_cmp = nsa_compress([segments(t) for t in outs[1:1 + G]], [segments(t) for t in outs[1 + G:1 + 2 * G]],
                                cmp_pos_k, cmp_w1_k, cmp_w2_k, cmp_pos_v, cmp_w1_v, cmp_w2_v)
    oT = nsa_attention_t(qT, gT, k_cmp, v_cmp, k_s, vT_s, k_w, vT_w, B, S)
    out = proj_residual(h2, [oT], [w_out.astype(BF16)], transposed=True)
    return out.reshape(B, S, D)


def kernel(x, norm_mix, norm_ffn, norm_final, hy_w_in, gdn_conv, gdn_a_log, gdn_dt_bias, gdn_norm, ssd_conv, ssd_conv_bias, ssd_a_log, ssd_dt_bias, ssd_d, ssd_norm, hy_w_out, nsa_w_in, cmp_pos_k, cmp_w1_k, cmp_w2_k, cmp_pos_v, cmp_w1_v, cmp_w2_v, nsa_w_out, ffn_w_up, ffn_conv, ffn_conv_bias, ffn_w_down):
    depth = norm_mix.shape[0]
    h = x
    for layer in range(depth):
        e = layer // 2
        if layer % 2 == 0:
            h = hybrid_layer(h, norm_mix[layer], hy_w_in[e], gdn_conv[e], gdn_a_log[e], gdn_dt_bias[e], gdn_norm[e],
                             ssd_conv[e], ssd_conv_bias[e], ssd_a_log[e], ssd_dt_bias[e], ssd_d[e], ssd_norm[e],
                             hy_w_out[e])
        else:
            h = nsa_layer(h, norm_mix[layer], nsa_w_in[e], cmp_pos_k[e], cmp_w1_k[e], cmp_w2_k[e], cmp_pos_v[e],
                          cmp_w1_v[e], cmp_w2_v[e], nsa_w_out[e])
        h = conv_ffn(h, norm_ffn[layer], ffn_w_up[layer], ffn_conv[layer], ffn_conv_bias[layer], ffn_w_down[layer],
                     norm_final, layer == depth - 1)
    return h
```

```python
import functools
import math

import numpy as np
import jax
import jax.numpy as jnp
from jax import lax
from jax.experimental import pallas as pl
from jax.experimental.pallas import tpu as pltpu

F32 = jnp.float32
BF16 = jnp.bfloat16
HIGHEST = lax.Precision.HIGHEST

EPS = 1e-6
LANES = 128
VMEM_LIMIT = 56 * 1024 * 1024

GDN_HEADS = 4
GDN_HEAD_DIM = 128
GDN_WIDTH = GDN_HEADS * GDN_HEAD_DIM
CHUNK = 64
SHORT_CONV = 4
SSD_HEADS = 8
SSD_HEAD_DIM = 64
SSD_WIDTH = SSD_HEADS * SSD_HEAD_DIM
SSD_GROUPS = 2
SSD_STATE = 128
SSD_XBC = SSD_WIDTH + 2 * SSD_GROUPS * SSD_STATE
NSA_HEADS = 16
NSA_KV_GROUPS = 2
NSA_HEAD_DIM = 64
NSA_KV = NSA_KV_GROUPS * NSA_HEAD_DIM
NSA_J = NSA_HEADS // NSA_KV_GROUPS
CMP_BLOCK = 32
CMP_STRIDE = 16
CMP_HIDDEN = 256
SEL_BLOCK = 64
SEL_TOPK = 6
WINDOW = 512
SEL_FORCE = 1e9
NEG_BIG = -1e30
FFN_CONV = 3
FF_CHUNK = 256
DOWN_GROUP = 6


def _cparams(*sem):
    return pltpu.CompilerParams(dimension_semantics=sem, vmem_limit_bytes=VMEM_LIMIT)


def _const_spec(shape):
    nd = len(shape)
    return pl.BlockSpec(shape, lambda *_: (0,) * nd, pipeline_mode=pl.Buffered(1))


def _rms(x, w):
    return x * lax.rsqrt(jnp.mean(x * x, axis=-1, keepdims=True) + EPS) * w


def _silu(x):
    return x * (1.0 / (1.0 + jnp.exp(-x)))


def _softplus(x):
    return jnp.maximum(x, 0.0) + jnp.log1p(jnp.exp(-jnp.abs(x)))


def _bdot(a, b):
    return jnp.dot(a.astype(BF16), b.astype(BF16), preferred_element_type=F32)


def _bdot_nt(a, b):
    return lax.dot_general(a.astype(BF16), b.astype(BF16), (((1,), (1,)), ((), ())), preferred_element_type=F32)


def _bdot_tn(a, b):
    return lax.dot_general(a.astype(BF16), b.astype(BF16), (((0,), (0,)), ((), ())), preferred_element_type=F32)


def _fdot(a, b):
    return jnp.dot(a, b, preferred_element_type=F32, precision=HIGHEST)


def _norm_proj_kernel(x_ref, nw_ref, *refs, transposed):
    n_out = len(transposed)
    xn = _rms(x_ref[...], nw_ref[...]).astype(BF16)
    for w_ref, o_ref, tr in zip(refs[:n_out], refs[n_out:], transposed):
        if tr:
            res = lax.dot_general(w_ref[...], xn, (((1,), (1,)), ((), ())), preferred_element_type=F32)
        else:
            res = jnp.dot(xn, w_ref[...], preferred_element_type=F32)
        o_ref[...] = res.astype(o_ref.dtype)


def norm_proj(h2, nw, weights, transposed=None, tm=512):
    T, D = h2.shape
    transposed = tuple(transposed or (False,) * len(weights))
    ws = [(w.T if tr else w).astype(BF16) for w, tr in zip(weights, transposed)]
    out_specs, out_shape = [], []
    for w, tr in zip(weights, transposed):
        n = w.shape[1]
        out_specs.append(pl.BlockSpec((n, tm), lambda i: (0, i)) if tr else pl.BlockSpec((tm, n), lambda i: (i, 0)))
        out_shape.append(jax.ShapeDtypeStruct((n, T) if tr else (T, n), F32))
    return pl.pallas_call(
        functools.partial(_norm_proj_kernel, transposed=transposed),
        grid=(T // tm,),
        in_specs=[pl.BlockSpec((tm, D), lambda i: (i, 0)), _const_spec((1, D))] + [_const_spec(w.shape) for w in ws],
        out_specs=out_specs,
        out_shape=out_shape,
        compiler_params=_cparams("parallel"),
        name="norm_proj",
    )(h2, nw.reshape(1, D), *ws)


def _proj_residual_kernel(h_ref, *refs, n_in, transposed):
    acc = h_ref[...]
    for a_ref, w_ref in zip(refs[:n_in], refs[n_in:2 * n_in]):
        if transposed:
            acc = acc + _bdot_tn(a_ref[...], w_ref[...])
        else:
            acc = acc + jnp.dot(a_ref[...].astype(BF16), w_ref[...], preferred_element_type=F32)
    refs[2 * n_in][...] = acc


def proj_residual(h2, acts, weights, transposed=False, tm=512):
    T, D = h2.shape
    n_in = len(acts)
    if transposed:
        act_specs = [pl.BlockSpec((a.shape[0], tm), lambda i: (0, i)) for a in acts]
    else:
        act_specs = [pl.BlockSpec((tm, a.shape[1]), lambda i: (i, 0)) for a in acts]
    return pl.pallas_call(
        functools.partial(_proj_residual_kernel, n_in=n_in, transposed=transposed),
        grid=(T // tm,),
        in_specs=[pl.BlockSpec((tm, D), lambda i: (i, 0))] + act_specs
        + [_const_spec(w.shape) for w in weights],
        out_specs=pl.BlockSpec((tm, D), lambda i: (i, 0)),
        out_shape=jax.ShapeDtypeStruct((T, D), F32),
        compiler_params=_cparams("parallel"),
        name="proj_residual",
    )(h2, *acts, *weights)


def _ffn_kernel(h_ref, nw_ref, wup_ref, cw_ref, cb_ref, wd_ref, fw_ref, o_ref,
                xn_ref, acc_ref, act_ref, pg_ref, pu_ref, *, n_chunks, final_norm):
    tm = h_ref.shape[0]
    dff = wd_ref.shape[0]
    gate_cols = lambda c: slice(c * FF_CHUNK, (c + 1) * FF_CHUNK)
    up_cols = lambda c: slice(dff + c * FF_CHUNK, dff + (c + 1) * FF_CHUNK)

    @pl.when(pl.program_id(1) == 0)
    def _():
        pg_ref[...] = jnp.zeros_like(pg_ref)
        pu_ref[...] = jnp.zeros_like(pu_ref)

    x = h_ref[...]
    xn_ref[...] = _rms(x, nw_ref[...]).astype(BF16)
    acc_ref[...] = x
    row = lax.broadcasted_iota(jnp.int32, (tm, 1), 0)

    def conv(u, prev, w, b):
        m1 = jnp.where(row == 0, prev[7:8], pltpu.roll(u, 1, 0))
        m2 = jnp.where(row == 0, prev[6:7], jnp.where(row == 1, prev[7:8], pltpu.roll(u, 2, 0)))
        return u * w[2:3] + m1 * w[1:2] + m2 * w[0:1] + b

    def up(c):
        xn = xn_ref[...]
        return (jnp.dot(xn, wup_ref[:, gate_cols(c)], preferred_element_type=F32),
                jnp.dot(xn, wup_ref[:, up_cols(c)], preferred_element_type=F32))

    nxt = up(0)
    for c in range(n_chunks):
        ug, uu = nxt
        if c + 1 < n_chunks:
            nxt = up(c + 1)
        yg = conv(ug, pg_ref[c], cw_ref[:, gate_cols(c)], cb_ref[:, gate_cols(c)])
        yu = conv(uu, pu_ref[c], cw_ref[:, up_cols(c)], cb_ref[:, up_cols(c)])
        pg_ref[c] = ug[tm - 8:tm]
        pu_ref[c] = uu[tm - 8:tm]
        act_ref[:, gate_cols(c)] = (_silu(yg) * yu).astype(BF16)
        if (c + 1) % DOWN_GROUP == 0 or c + 1 == n_chunks:
            rows = slice((c // DOWN_GROUP) * DOWN_GROUP * FF_CHUNK, (c + 1) * FF_CHUNK)
            acc_ref[...] += jnp.dot(act_ref[:, rows], wd_ref[rows, :], preferred_element_type=F32)
    out = acc_ref[...]
    if final_norm:
        out = _rms(out, fw_ref[...])
    o_ref[...] = out


def conv_ffn(h3, nw, w_up, conv_w, conv_b, w_down, final_w, final_norm, tm=512):
    B, S, D = h3.shape
    dff = w_down.shape[0]
    nch = dff // FF_CHUNK
    consts = (w_up.astype(BF16), conv_w, conv_b.reshape(1, 2 * dff), w_down.astype(BF16))
    tile = pl.BlockSpec((None, tm, D), lambda b, s: (b, s, 0))
    return pl.pallas_call(
        functools.partial(_ffn_kernel, n_chunks=nch, final_norm=final_norm),
        grid=(B, S // tm),
        in_specs=[tile, _const_spec((1, D))] + [_const_spec(a.shape) for a in consts] + [_const_spec((1, D))],
        out_specs=tile,
        out_shape=jax.ShapeDtypeStruct((B, S, D), F32),
        scratch_shapes=[pltpu.VMEM((tm, D), BF16), pltpu.VMEM((tm, D), F32), pltpu.VMEM((tm, dff), BF16),
                        pltpu.VMEM((nch, 8, FF_CHUNK), F32), pltpu.VMEM((nch, 8, FF_CHUNK), F32)],
        compiler_params=_cparams("parallel", "arbitrary"),
        name="conv_ffn",
    )(h3, nw.reshape(1, D), *consts, final_w.reshape(1, D))


def _causal_conv(x, xbuf, w, ts, width):
    xbuf[8:8 + ts, :] = x
    y = x * w[width - 1:width]
    for k in range(1, width):
        y = y + xbuf[pl.ds(8 - k, ts), :] * w[width - 1 - k:width - k]
    xbuf[0:8, :] = x[ts - 8:ts]
    return y


def _gdn_kernel(qkv_ref, sm_ref, z_ref, cw_ref, alog_ref, dtb_ref, gn_ref, o_ref,
                xbuf, qn, kn, vn, gb, state, *, ts):
    nh, dk, C = GDN_HEADS, GDN_HEAD_DIM, CHUNK
    R = nh * C

    @pl.when(pl.program_id(1) == 0)
    def _():
        xbuf[0:8, :] = jnp.zeros((8, xbuf.shape[1]), F32)
        state[...] = jnp.zeros_like(state)

    y = _silu(_causal_conv(qkv_ref[...], xbuf, cw_ref[...], ts, SHORT_CONV))
    for h in range(nh):
        qh = y[:, h * dk:(h + 1) * dk]
        qn[h] = qh * (lax.rsqrt(jnp.sum(qh * qh, axis=-1, keepdims=True) + EPS) * dk ** -0.5)
        kh = y[:, GDN_WIDTH + h * dk:GDN_WIDTH + (h + 1) * dk]
        kn[h] = kh * lax.rsqrt(jnp.sum(kh * kh, axis=-1, keepdims=True) + EPS)
        vn[h] = y[:, 2 * GDN_WIDTH + h * dk:2 * GDN_WIDTH + (h + 1) * dk]
    sm = sm_ref[...]
    lane = lax.broadcasted_iota(jnp.int32, sm.shape, 1)
    beta = 1.0 / (1.0 + jnp.exp(-sm))
    g = -jnp.exp(alog_ref[...]) * _softplus(sm + dtb_ref[...])
    gb[...] = jnp.where(lane < nh, beta, g)

    ri = lax.broadcasted_iota(jnp.int32, (R, R), 0)
    ci = lax.broadcasted_iota(jnp.int32, (R, R), 1)
    same_head = (ri >> 6) == (ci >> 6)
    causal = same_head & (ri >= ci)
    strict = same_head & (ri > ci)
    eye = (ri == ci).astype(F32)
    l64 = (lax.broadcasted_iota(jnp.int32, (C, C), 0) >= lax.broadcasted_iota(jnp.int32, (C, C), 1)).astype(F32)
    gnw = gn_ref[...]

    def stack(ref, r0):
        return jnp.concatenate([ref[h, pl.ds(r0, C), :] for h in range(nh)], axis=0)

    chunks = range(ts // C)
    each = lambda fn: [fn(c) for c in chunks]
    gbc = each(lambda c: gb[c * C:(c + 1) * C, :])
    gam = each(lambda c: _fdot(l64, gbc[c]))
    q = each(lambda c: stack(qn, c * C))
    k = each(lambda c: stack(kn, c * C))
    v = each(lambda c: stack(vn, c * C))
    beta_col = each(lambda c: jnp.concatenate([gbc[c][:, h:h + 1] for h in range(nh)], axis=0))
    gam_col = each(lambda c: jnp.concatenate([gam[c][:, nh + h:nh + h + 1] for h in range(nh)], axis=0))
    decay = each(lambda c: jnp.exp(jnp.where(
        causal, gam_col[c] - jnp.sum(eye * gam_col[c], axis=0, keepdims=True), NEG_BIG)))
    kb = each(lambda c: k[c] * beta_col[c])
    p = each(lambda c: jnp.where(strict, -(_bdot_nt(kb[c], k[c]) * decay[c]), 0.0))
    t = each(lambda c: eye + p[c])
    for _ in range(5):
        p = each(lambda c: _bdot(p[c], p[c]))
        t = each(lambda c: t[c] + _bdot(t[c], p[c]))
    eg = each(lambda c: jnp.exp(gam_col[c]))
    sol = each(lambda c: _bdot(t[c], jnp.concatenate([v[c] * beta_col[c], kb[c] * eg[c]], axis=1)))
    qk = each(lambda c: _bdot_nt(q[c], k[c]) * decay[c])
    gl = each(lambda c: [gam[c][C - 1:C, nh + h:nh + h + 1] for h in range(nh)])
    k_dec = each(lambda c: k[c] * jnp.exp(
        jnp.concatenate([jnp.broadcast_to(gl[c][h], (C, 1)) for h in range(nh)], axis=0) - gam_col[c]))
    wq = each(lambda c: [jnp.concatenate([sol[c][h * C:(h + 1) * C, dk:], (q[c] * eg[c])[h * C:(h + 1) * C]], axis=0)
                         for h in range(nh)])

    heads = range(nh)
    s_cur = [state[h] for h in heads]
    for c in chunks:
        ws = [_bdot(wq[c][h], s_cur[h]) for h in heads]
        u = [sol[c][h * C:(h + 1) * C, :dk] - ws[h][:C] for h in heads]
        o = [ws[h][C:] + _bdot(qk[c][h * C:(h + 1) * C, h * C:(h + 1) * C], u[h]) for h in heads]
        s_cur = [s_cur[h] * jnp.exp(gl[c][h]) + _bdot_tn(k_dec[c][h * C:(h + 1) * C], u[h]) for h in heads]
        for h in heads:
            zh = z_ref[c * C:(c + 1) * C, h * dk:(h + 1) * dk]
            o_ref[c * C:(c + 1) * C, h * dk:(h + 1) * dk] = _rms(o[h], gnw) * _silu(zh)
    for h in heads:
        state[h] = s_cur[h]


def gdn_heads(qkv, small, z_a, conv_w, a_log, dt_bias, gnorm, ts=256):
    B, S, _ = qkv.shape
    nh, dk = GDN_HEADS, GDN_HEAD_DIM
    alog_l = jnp.zeros((1, LANES), F32).at[0, nh:2 * nh].set(a_log)
    dtb_l = jnp.zeros((1, LANES), F32).at[0, nh:2 * nh].set(dt_bias)

    def tile(n):
        return pl.BlockSpec((None, ts, n), lambda b, s: (b, s, 0))

    return pl.pallas_call(
        functools.partial(_gdn_kernel, ts=ts),
        grid=(B, S // ts),
        in_specs=[tile(3 * GDN_WIDTH), tile(LANES), tile(GDN_WIDTH), _const_spec(conv_w.shape),
                  _const_spec((1, LANES)), _const_spec((1, LANES)), _const_spec((1, dk))],
        out_specs=tile(GDN_WIDTH),
        out_shape=jax.ShapeDtypeStruct((B, S, GDN_WIDTH), F32),
        scratch_shapes=[pltpu.VMEM((ts + 8, 3 * GDN_WIDTH), F32)]
        + [pltpu.VMEM((nh, ts, dk), F32)] * 3
        + [pltpu.VMEM((ts, LANES), F32), pltpu.VMEM((nh, dk, dk), F32)],
        compiler_params=_cparams("parallel", "arbitrary"),
        name="gdn_heads",
    )(qkv, small, z_a, conv_w, alog_l, dtb_l, gnorm.reshape(1, dk))


def _ssd_kernel(xbc_ref, sm_ref, z_ref, cw_ref, cb_ref, dtb_ref, alog_ref, d_ref, nw_ref, e_ref, o_ref,
                xbuf, xs_s, b_s, c_s, dt_s, hstate, *, ts):
    C, P, N = CHUNK, SSD_HEAD_DIM, SSD_STATE
    hpg = SSD_HEADS // SSD_GROUPS
    gw = hpg * P

    @pl.when(pl.program_id(1) == 0)
    def _():
        xbuf[0:8, :] = jnp.zeros((8, xbuf.shape[1]), F32)
        hstate[...] = jnp.zeros_like(hstate)

    y = _silu(_causal_conv(xbc_ref[...], xbuf, cw_ref[...], ts, SHORT_CONV) + cb_ref[...])
    xs_s[...] = y[:, :SSD_WIDTH]
    b_s[...] = y[:, SSD_WIDTH:SSD_WIDTH + SSD_GROUPS * N]
    c_s[...] = y[:, SSD_WIDTH + SSD_GROUPS * N:]
    sm = sm_ref[...]
    lane = lax.broadcasted_iota(jnp.int32, sm.shape, 1)
    dt = jnp.where((lane >= SSD_HEADS) & (lane < 2 * SSD_HEADS), _softplus(sm + dtb_ref[...]), 0.0)
    dt_s[...] = _fdot(dt, e_ref[...])
    a_l = -jnp.exp(alog_ref[...])
    d_l = d_ref[...]
    nw = nw_ref[...]

    li = lax.broadcasted_iota(jnp.int32, (C, C), 0)
    si = lax.broadcasted_iota(jnp.int32, (C, C), 1)
    causal = li >= si
    l64 = causal.astype(F32)
    eye = (li == si).astype(F32)

    chunks = range(ts // C)
    groups = range(SSD_GROUPS)
    each = lambda fn: [fn(c) for c in chunks]
    each_g = lambda fn: [[fn(c, g) for g in groups] for c in chunks]
    gl = lambda g: slice(g * gw, (g + 1) * gw)
    rows = lambda c: slice(c * C, (c + 1) * C)
    dtc = each(lambda c: dt_s[rows(c), :])
    xs = each(lambda c: xs_s[rows(c), :])
    xc = each(lambda c: xs[c] * dtc[c])
    acs = each(lambda c: _fdot(l64, dtc[c] * a_l))
    eacs = each(lambda c: jnp.exp(acs[c]))
    xd = each(lambda c: xc[c] * jnp.exp(acs[c][C - 1:C, :] - acs[c]))
    e_last = each(lambda c: jnp.exp(acs[c][C - 1:C, :]))
    bg = each_g(lambda c, g: b_s[rows(c), g * N:(g + 1) * N])
    cg = each_g(lambda c, g: c_s[rows(c), g * N:(g + 1) * N])
    cb = each_g(lambda c, g: _bdot_nt(cg[c][g], bg[c][g]))
    s_inc = each_g(lambda c, g: _bdot_tn(bg[c][g], xd[c][:, gl(g)]))

    def diag(c, g):
        yd = []
        for j in range(hpg):
            hl = slice((g * hpg + j) * P, (g * hpg + j + 1) * P)
            blk = acs[c][:, hl]
            row = jnp.sum(blk * eye, axis=0, keepdims=True)
            m = cb[c][g] * jnp.exp(jnp.where(causal, blk - row, NEG_BIG))
            yd.append(_bdot(m, xc[c][:, hl]))
        return jnp.concatenate(yd, axis=1)

    y_diag = each_g(diag)

    h_cur = [hstate[g] for g in groups]
    for c in chunks:
        y_off = [_bdot(cg[c][g], h_cur[g]) * eacs[c][:, gl(g)] for g in groups]
        h_cur = [h_cur[g] * e_last[c][:, gl(g)] + s_inc[c][g] for g in groups]
        yy = jnp.concatenate([y_diag[c][g] + y_off[g] for g in groups], axis=1) + d_l * xs[c]
        yy = yy * _silu(z_ref[rows(c), :])
        o_ref[rows(c), :] = jnp.concatenate([_rms(yy[:, gl(g)], nw[:, gl(g)]) for g in groups], axis=1)
    for g in groups:
        hstate[g] = h_cur[g]


def ssd_heads(xbc, small, z_b, conv_w, conv_b, a_log, dt_bias, d_skip, norm_w, ts=256):
    B, S, _ = xbc.shape
    nh, P = SSD_HEADS, SSD_HEAD_DIM
    dtb_l = jnp.zeros((1, LANES), F32).at[0, nh:2 * nh].set(dt_bias)
    expand = np.zeros((LANES, SSD_WIDTH), np.float32)
    for h in range(nh):
        expand[nh + h, h * P:(h + 1) * P] = 1.0

    def tile(n):
        return pl.BlockSpec((None, ts, n), lambda b, s: (b, s, 0))

    gw = SSD_WIDTH // SSD_GROUPS
    return pl.pallas_call(
        functools.partial(_ssd_kernel, ts=ts),
        grid=(B, S // ts),
        in_specs=[tile(SSD_XBC), tile(LANES), tile(SSD_WIDTH), _const_spec(conv_w.shape), _const_spec((1, SSD_XBC)),
                  _const_spec((1, LANES)), _const_spec((1, SSD_WIDTH)), _const_spec((1, SSD_WIDTH)),
                  _const_spec((1, SSD_WIDTH)), _const_spec((LANES, SSD_WIDTH))],
        out_specs=tile(SSD_WIDTH),
        out_shape=jax.ShapeDtypeStruct((B, S, SSD_WIDTH), F32),
        scratch_shapes=[pltpu.VMEM((ts + 8, SSD_XBC), F32), pltpu.VMEM((ts, SSD_WIDTH), F32),
                        pltpu.VMEM((ts, SSD_GROUPS * SSD_STATE), F32), pltpu.VMEM((ts, SSD_GROUPS * SSD_STATE), F32),
                        pltpu.VMEM((ts, SSD_WIDTH), F32), pltpu.VMEM((SSD_GROUPS, SSD_STATE, gw), F32)],
        compiler_params=_cparams("parallel", "arbitrary"),
        name="ssd_heads",
    )(xbc, small, z_b, conv_w, conv_b.reshape(1, SSD_XBC), dtb_l, jnp.repeat(a_log, P).reshape(1, SSD_WIDTH),
      jnp.repeat(d_skip, P).reshape(1, SSD_WIDTH), norm_w.reshape(1, SSD_WIDTH), jnp.asarray(expand))


def hybrid_layer(h, nw, w_in, gdn_conv, gdn_a_log, gdn_dt_bias, gdn_norm, ssd_conv, ssd_conv_bias, ssd_a_log,
                 ssd_dt_bias, ssd_d, ssd_norm, w_out):
    B, S, D = h.shape
    T = B * S
    gwid, swid = GDN_WIDTH, SSD_WIDTH
    o_za = 3 * gwid
    o_ba = o_za + gwid
    o_zb = o_ba + 2 * GDN_HEADS
    o_xbc = o_zb + swid
    o_dt = o_xbc + SSD_XBC
    w_small = jnp.zeros((D, LANES), F32)
    w_small = w_small.at[:, :2 * GDN_HEADS].set(w_in[:, o_ba:o_zb])
    w_small = w_small.at[:, SSD_HEADS:2 * SSD_HEADS].set(w_in[:, o_dt:o_dt + SSD_HEADS])
    ws = [w_in[:, :o_za], w_in[:, o_za:o_ba], w_small, w_in[:, o_zb:o_xbc], w_in[:, o_xbc:o_dt]]
    h2 = h.reshape(T, D)
    qkv, z_a, small, z_b, xbc = norm_proj(h2, nw, [w.astype(BF16) for w in ws])
    small3 = small.reshape(B, S, LANES)
    o_a = gdn_heads(qkv.reshape(B, S, -1), small3, z_a.reshape(B, S, -1), gdn_conv, gdn_a_log, gdn_dt_bias, gdn_norm)
    y_b = ssd_heads(xbc.reshape(B, S, -1), small3, z_b.reshape(B, S, -1), ssd_conv, ssd_conv_bias, ssd_a_log,
                    ssd_dt_bias, ssd_d, ssd_norm)
    out = proj_residual(h2, [o_a.reshape(T, gwid), y_b.reshape(T, swid)],
                        [w_out[:gwid].astype(BF16), w_out[gwid:].astype(BF16)])
    return out.reshape(B, S, D)


def _compress_kernel(kc0_ref, kc1_ref, vc0_ref, vc1_ref, pk0_ref, pk1_ref, pv0_ref, pv1_ref,
                     w1ka_ref, w1kb_ref, w1va_ref, w1vb_ref, w2k_ref, w2v_ref, ko_ref, vo_ref):
    n_seg = kc0_ref.shape[0]
    for srcs, p0, p1, wa, wb, w2, out in (((kc0_ref, kc1_ref), pk0_ref, pk1_ref, w1ka_ref, w1kb_ref, w2k_ref, ko_ref),
                                          ((vc0_ref, vc1_ref), pv0_ref, pv1_ref, w1va_ref, w1vb_ref, w2v_ref, vo_ref)):
        for g in range(NSA_KV_GROUPS):
            t = srcs[g][...]
            f0 = _bdot(t + p0[...], wa[...])
            f1 = _bdot(t + p1[...], wb[...])
            hid = f0 + pltpu.roll(f1, n_seg - 1, 0)
            out[g] = _bdot(_silu(hid), w2[...])


def nsa_compress(kc_groups, vc_groups, pos_k, w1_k, w2_k, pos_v, w1_v, w2_v):
    B, n_seg, seg_w = kc_groups[0].shape
    G = NSA_KV_GROUPS
    half = CMP_STRIDE
    args = [*kc_groups, *vc_groups,
            pos_k[:half].reshape(1, seg_w), pos_k[half:].reshape(1, seg_w),
            pos_v[:half].reshape(1, seg_w), pos_v[half:].reshape(1, seg_w),
            w1_k[:seg_w].astype(BF16), w1_k[seg_w:].astype(BF16), w1_v[:seg_w].astype(BF16), w1_v[seg_w:].astype(BF16),
            w2_k.astype(BF16), w2_v.astype(BF16)]
    per_b = pl.BlockSpec((None, n_seg, seg_w), lambda b: (b, 0, 0))
    out_b = pl.BlockSpec((None, G, n_seg, NSA_HEAD_DIM), lambda b: (b, 0, 0, 0))
    return pl.pallas_call(
        _compress_kernel,
        grid=(B,),
        in_specs=[per_b] * (2 * G) + [_const_spec(a.shape) for a in args[2 * G:]],
        out_specs=[out_b, out_b],
        out_shape=[jax.ShapeDtypeStruct((B, G, n_seg, NSA_HEAD_DIM), F32)] * 2,
        compiler_params=_cparams("parallel"),
        name="nsa_compress",
    )(*args)


def _alibi_slopes():
    return [[float(2.0 ** (-8.0 * (g * NSA_J + j + 1) / NSA_HEADS)) for j in range(NSA_J)]
            for g in range(NSA_KV_GROUPS)]


M_INIT = -1e20


TQ = LANES
BF16_SUBLANES = 16
AUG = 128
POS_ROW0 = NSA_HEAD_DIM
SEL_ROW0 = POS_ROW0 + BF16_SUBLANES
ZERO_ROWS = BF16_SUBLANES
V_ROWS = NSA_HEAD_DIM + BF16_SUBLANES
POS_BITS = 6
N_SPLIT = 5
SEL_SHIFT = SEL_BLOCK.bit_length() - 1
LOG2E = math.log2(math.e)


def _slope_rows():
    G, J = NSA_KV_GROUPS, NSA_J
    rows = np.zeros((G, SEL_ROW0 - POS_ROW0, J * TQ), np.float32)
    for g, per_group in enumerate(_alibi_slopes()):
        for j, slope in enumerate(per_group):
            rest = float(np.float32(slope)) * LOG2E
            for i in range(N_SPLIT):
                piece = float(np.asarray(rest, dtype=BF16))
                rows[g, i, j * TQ:(j + 1) * TQ] = piece * 2.0 ** POS_BITS
                rows[g, N_SPLIT + i, j * TQ:(j + 1) * TQ] = piece
                rest -= piece
    return jnp.asarray(rows, BF16)


def _nsa_t_kernel(qT_ref, gT_ref, kc_ref, vc_ref, ks_ref, vsT_ref, kw_ref, vwT_ref, covT_ref, slope_ref, o_ref,
                  kaug_s, kaug_w, vaug_s, vaug_w, qaug, oc_ref, m_ref, acc_ref, *, tk, seq, n_cmp, n_sel):
    G, J, Dh, tq = NSA_KV_GROUPS, NSA_J, NSA_HEAD_DIM, TQ
    slopes = _alibi_slopes()
    qi = pl.program_id(1)
    start = qi * tq
    nc = kc_ref.shape[1]
    nsp = AUG - SEL_ROW0 - ZERO_ROWS
    top_k = min(SEL_TOPK, n_sel)
    lanes = J * tq

    @pl.when(qi == 0)
    def _():
        pos = lax.broadcasted_iota(jnp.int32, (seq, AUG - Dh), 0)
        col = lax.broadcasted_iota(jnp.int32, (seq, AUG - Dh), 1) + Dh
        pos_cols = jnp.where(col < POS_ROW0 + N_SPLIT, pos >> POS_BITS,
                             jnp.where(col < POS_ROW0 + 2 * N_SPLIT, pos & (2 ** POS_BITS - 1), 0)).astype(F32)
        onehot = ((col >= SEL_ROW0) & ((pos >> SEL_SHIFT) == col - SEL_ROW0)).astype(F32)
        ones_row = (lax.broadcasted_iota(jnp.int32, (V_ROWS - Dh, seq), 0) == 0).astype(F32)
        for g in range(G):
            for k_ref, ka, vT_ref, va, extra in ((ks_ref, kaug_s, vsT_ref, vaug_s, pos_cols + onehot),
                                                 (kw_ref, kaug_w, vwT_ref, vaug_w, pos_cols)):
                ka[g] = jnp.concatenate([k_ref[:, g * Dh:(g + 1) * Dh], extra], axis=1).astype(BF16)
                va[g] = jnp.concatenate([vT_ref[g * Dh:(g + 1) * Dh, :], ones_row], axis=0).astype(BF16)

    t_row = start + lax.broadcasted_iota(jnp.int32, (1, tq), 1)
    gates = 1.0 / (1.0 + jnp.exp(-gT_ref[...]))

    def per_head(x, fn):
        return jnp.concatenate([fn(j, x[:, j * tq:(j + 1) * tq]) for j in range(J)], axis=1)

    for g in range(G):
        qT = jnp.concatenate([qT_ref[(g * J + j) * Dh:(g * J + j + 1) * Dh, :] for j in range(J)], axis=1) * Dh ** -0.5

        n_idx = lax.broadcasted_iota(jnp.int32, (nc, tq), 0)
        dist_c = t_row - (n_idx * CMP_STRIDE + (CMP_BLOCK - 1))
        pen_c = jnp.where((dist_c >= 0) & (n_idx < n_cmp), 0.0, -NEG_BIG)
        distf_c = dist_c.astype(F32)
        s_c = per_head(_bdot(kc_ref[g], qT), lambda j, sj: sj - (slopes[g][j] * distf_c + pen_c))
        m_c = jnp.maximum(jnp.max(s_c, axis=0, keepdims=True), M_INIT)
        e_c = jnp.exp(s_c - m_c)
        p_c = e_c / jnp.maximum(jnp.sum(e_c, axis=0, keepdims=True), 1e-30)
        oc_ref[g] = _bdot_tn(vc_ref[g], p_c)
        p_sum = p_c[:, 0:tq]
        for j in range(1, J):
            p_sum = p_sum + p_c[:, j * tq:(j + 1) * tq]
        imp = _fdot(covT_ref[...], p_sum)

        blk = lax.broadcasted_iota(jnp.int32, (nsp, tq), 0)
        cur = t_row >> SEL_SHIFT
        forced = (blk == 0) | (blk == cur) | (blk == cur - 1)
        imp = jnp.where(forced, SEL_FORCE, jnp.where(blk * SEL_BLOCK <= t_row, imp, -SEL_FORCE))
        imp = jnp.where(blk < n_sel, imp, -3e38)
        sel_pen = jnp.full((nsp, tq), NEG_BIG, F32)
        for _ in range(top_k):
            mx = jnp.max(imp, axis=0, keepdims=True)
            first = jnp.min(jnp.where(imp == mx, blk, nsp), axis=0, keepdims=True)
            hit = blk == first
            sel_pen = jnp.where(hit, 0.0, sel_pen)
            imp = jnp.where(hit, -jnp.inf, imp)

        qaug[g] = jnp.concatenate(
            [(qT * LOG2E).astype(BF16), slope_ref[g], jnp.concatenate([sel_pen] * J, axis=1).astype(BF16),
             jnp.zeros((ZERO_ROWS, lanes), BF16)], axis=0)

    def attend(tile, k0_of, kaug, vaug):
        m_ref[...] = jnp.full(m_ref.shape, M_INIT, F32)
        acc_ref[...] = jnp.zeros(acc_ref.shape, F32)

        def scores(kt):
            k0 = pl.multiple_of(k0_of(kt), tile)
            return [jnp.dot(kaug[g, pl.ds(k0, tile), :], qaug[g], preferred_element_type=F32) for g in range(G)]

        def update(kt, sc, valid_fn):
            k0 = pl.multiple_of(k0_of(kt), tile)
            if valid_fn is not None:
                pos = k0 + lax.broadcasted_iota(jnp.int32, (tile, tq), 0)
                mask_pen = jnp.where(valid_fn(t_row - pos), 0.0, NEG_BIG)
            for g in range(G):
                s = sc[g]
                if valid_fn is not None:
                    s = per_head(s, lambda j, sj: sj + mask_pen)
                m_old = m_ref[g]
                m_new = jnp.maximum(m_old, jnp.max(s, axis=0, keepdims=True))
                e = jnp.exp2(s - m_new).astype(BF16)
                pv = jnp.dot(vaug[g, :, pl.ds(k0, tile)], e, preferred_element_type=F32)
                acc_ref[g] = jnp.exp2(m_old - m_new) * acc_ref[g] + pv
                m_ref[g] = m_new

        return scores, update

    def finish(g):
        acc = acc_ref[g]
        return acc[0:Dh] / jnp.maximum(acc[Dh:Dh + 1], 1e-30)

    n_past = (start + tq + tk - 1) // tk - 1
    sel_scores, sel_update = attend(tk, lambda kt: kt * tk, kaug_s, vaug_s)
    diag_valid = lambda dist: dist >= 0

    def past_pair(p, carry):
        s0 = sel_scores(2 * p)
        s1 = sel_scores(2 * p + 1)
        sel_update(2 * p, s0, None)
        sel_update(2 * p + 1, s1, None)
        return carry

    lax.fori_loop(0, n_past // 2, past_pair, 0)

    @pl.when(n_past % 2 == 1)
    def _():
        s0 = sel_scores(n_past - 1)
        s1 = sel_scores(n_past)
        sel_update(n_past - 1, s0, None)
        sel_update(n_past, s1, diag_valid)

    @pl.when(n_past % 2 == 0)
    def _():
        sel_update(n_past, sel_scores(n_past), diag_valid)

    o_s = [finish(g) for g in range(G)]

    band0 = jnp.clip(start - WINDOW, 0, seq - (WINDOW + tq))
    win_scores, win_update = attend(tq, lambda kt: band0 + kt * tq, kaug_w, vaug_w)
    n_win = (WINDOW + tq) // tq
    sc_next = win_scores(0)
    for kt in range(n_win):
        sc = sc_next
        if kt + 1 < n_win:
            sc_next = win_scores(kt + 1)
        win_update(kt, sc, lambda dist: (dist >= 0) & (dist < WINDOW))
    o_w = [finish(g) for g in range(G)]

    for g in range(G):
        def gate_row(branch):
            return jnp.concatenate([gates[branch * NSA_HEADS + g * J + j:branch * NSA_HEADS + g * J + j + 1, :]
                                    for j in range(J)], axis=1)

        out = gate_row(0) * oc_ref[g] + gate_row(1) * o_s[g] + gate_row(2) * o_w[g]
        for j in range(J):
            o_ref[(g * J + j) * Dh:(g * J + j + 1) * Dh, :] = out[:, j * tq:(j + 1) * tq].astype(o_ref.dtype)


def nsa_attention_t(qT, gT, k_cmp, v_cmp, k_s, vT_s, k_w, vT_w, B, S, tk=256):
    width, T = qT.shape
    tq = TQ
    n_seg = k_cmp.shape[2]
    n_cmp = (S - CMP_BLOCK) // CMP_STRIDE + 1
    n_sel = S // SEL_BLOCK
    nsp = AUG - SEL_ROW0 - ZERO_ROWS
    assert n_sel <= nsp and S >= WINDOW + tq and POS_ROW0 + 2 * N_SPLIT <= SEL_ROW0
    cs = np.arange(n_seg) * CMP_STRIDE
    ss = np.arange(n_sel) * SEL_BLOCK
    cover = np.clip(np.minimum(cs[:, None] + CMP_BLOCK, ss[None, :] + SEL_BLOCK)
                    - np.maximum(cs[:, None], ss[None, :]), 0, None) / CMP_STRIDE
    cover_t = np.zeros((nsp, n_seg), np.float32)
    cover_t[:n_sel, :n_cmp] = cover[:n_cmp].T
    slope_rows = _slope_rows()
    nq = S // tq

    def qtile(n):
        return pl.BlockSpec((n, tq), lambda b, s: (0, b * nq + s))

    per_b_cmp = pl.BlockSpec((None, NSA_KV_GROUPS, n_seg, NSA_HEAD_DIM), lambda b, s: (b, 0, 0, 0))
    per_b_k = pl.BlockSpec((S, NSA_KV), lambda b, s: (b, 0))
    per_b_vT = pl.BlockSpec((NSA_KV, S), lambda b, s: (0, b))
    G = NSA_KV_GROUPS
    return pl.pallas_call(
        functools.partial(_nsa_t_kernel, tk=tk, seq=S, n_cmp=n_cmp, n_sel=n_sel),
        grid=(B, nq),
        in_specs=[qtile(width), qtile(LANES), per_b_cmp, per_b_cmp, per_b_k, per_b_vT, per_b_k, per_b_vT,
                  _const_spec(cover_t.shape), _const_spec(slope_rows.shape)],
        out_specs=qtile(width),
        out_shape=jax.ShapeDtypeStruct((width, T), BF16),
        scratch_shapes=[pltpu.VMEM((G, S, AUG), BF16), pltpu.VMEM((G, S, AUG), BF16),
                        pltpu.VMEM((G, V_ROWS, S), BF16), pltpu.VMEM((G, V_ROWS, S), BF16),
                        pltpu.VMEM((G, AUG, NSA_J * tq), BF16), pltpu.VMEM((G, NSA_HEAD_DIM, NSA_J * tq), F32),
                        pltpu.VMEM((G, 1, NSA_J * tq), F32), pltpu.VMEM((G, V_ROWS, NSA_J * tq), F32)],
        compiler_params=_cparams("parallel", "arbitrary"),
        name="nsa_attention",
    )(qT, gT, k_cmp, v_cmp, k_s, vT_s, k_w, vT_w, jnp.asarray(cover_t), slope_rows)


def nsa_layer(h, nw, w_in, cmp_pos_k, cmp_w1_k, cmp_w2_k, cmp_pos_v, cmp_w1_v, cmp_w2_v, w_out):
    B, S, D = h.shape
    T = B * S
    width = NSA_HEADS * NSA_HEAD_DIM
    G, Dh = NSA_KV_GROUPS, NSA_HEAD_DIM
    w_gates = jnp.zeros((D, LANES), F32).at[:, :3 * NSA_HEADS].set(w_in[:, width + 6 * NSA_KV:])
    kv = [w_in[:, width + i * NSA_KV:width + (i + 1) * NSA_KV] for i in range(6)]
    per_group = lambda w: [w[:, g * Dh:(g + 1) * Dh] for g in range(G)]
    ws = [w_in[:, :width]] + per_group(kv[0]) + per_group(kv[1]) + kv[2:] + [w_gates]
    h2 = h.reshape(T, D)
    outs = norm_proj(h2, nw, ws, transposed=(True,) + (False,) * (2 * G) + (False, True, False, True, True))
    qT, (k_s, vT_s, k_w, vT_w, gT) = outs[0], outs[1 + 2 * G:]
    n_seg = S // CMP_STRIDE
    segments = lambda t: t.reshape(B, n_seg, CMP_STRIDE * Dh)
    k_cmp, v_cmp = nsa_compress([segments(t) for t in outs[1:1 + G]], [segments(t) for t in outs[1 + G:1 + 2 * G]],
                                cmp_pos_k, cmp_w1_k, cmp_w2_k, cmp_pos_v, cmp_w1_v, cmp_w2_v)
    oT = nsa_attention_t(qT, gT, k_cmp, v_cmp, k_s, vT_s, k_w, vT_w, B, S)
    out = proj_residual(h2, [oT], [w_out.astype(BF16)], transposed=True)
    return out.reshape(B, S, D)


def kernel(x, norm_mix, norm_ffn, norm_final, hy_w_in, gdn_conv, gdn_a_log, gdn_dt_bias, gdn_norm, ssd_conv, ssd_conv_bias, ssd_a_log, ssd_dt_bias, ssd_d, ssd_norm, hy_w_out, nsa_w_in, cmp_pos_k, cmp_w1_k, cmp_w2_k, cmp_pos_v, cmp_w1_v, cmp_w2_v, nsa_w_out, ffn_w_up, ffn_conv, ffn_conv_bias, ffn_w_down):
    depth = norm_mix.shape[0]
    h = x
    for layer in range(depth):
        e = layer // 2
        if layer % 2 == 0:
            h = hybrid_layer(h, norm_mix[layer], hy_w_in[e], gdn_conv[e], gdn_a_log[e], gdn_dt_bias[e], gdn_norm[e],
                             ssd_conv[e], ssd_conv_bias[e], ssd_a_log[e], ssd_dt_bias[e], ssd_d[e], ssd_norm[e],
                             hy_w_out[e])
        else:
            h = nsa_layer(h, norm_mix[layer], nsa_w_in[e], cmp_pos_k[e], cmp_w1_k[e], cmp_w2_k[e], cmp_pos_v[e],
                          cmp_w1_v[e], cmp_w2_v[e], nsa_w_out[e])
        h = conv_ffn(h, norm_ffn[layer], ffn_w_up[layer], ffn_conv[layer], ffn_conv_bias[layer], ffn_w_down[layer],
                     norm_final, layer == depth - 1)
    return h
```

```python
import functools
import math

import numpy as np
import jax
import jax.numpy as jnp
from jax import lax
from jax.experimental import pallas as pl
from jax.experimental.pallas import tpu as pltpu

F32 = jnp.float32
BF16 = jnp.bfloat16
HIGHEST = lax.Precision.HIGHEST

EPS = 1e-6
LANES = 128
VMEM_LIMIT = 56 * 1024 * 1024

GDN_HEADS = 4
GDN_HEAD_DIM = 128
GDN_WIDTH = GDN_HEADS * GDN_HEAD_DIM
CHUNK = 64
SHORT_CONV = 4
GDN_INTERLEAVE = 4
SSD_HEADS = 8
SSD_HEAD_DIM = 64
SSD_WIDTH = SSD_HEADS * SSD_HEAD_DIM
SSD_GROUPS = 2
SSD_STATE = 128
SSD_XBC = SSD_WIDTH + 2 * SSD_GROUPS * SSD_STATE
NSA_HEADS = 16
NSA_KV_GROUPS = 2
NSA_HEAD_DIM = 64
NSA_KV = NSA_KV_GROUPS * NSA_HEAD_DIM
NSA_J = NSA_HEADS // NSA_KV_GROUPS
CMP_BLOCK = 32
CMP_STRIDE = 16
CMP_HIDDEN = 256
SEL_BLOCK = 64
SEL_TOPK = 6
WINDOW = 512
SEL_FORCE = 1e9
NEG_BIG = -1e30
FFN_CONV = 3
FF_CHUNK = 256
DOWN_GROUP = 6


def _cparams(*sem):
    return pltpu.CompilerParams(dimension_semantics=sem, vmem_limit_bytes=VMEM_LIMIT)


def _const_spec(shape):
    nd = len(shape)
    return pl.BlockSpec(shape, lambda *_: (0,) * nd, pipeline_mode=pl.Buffered(1))


def _rms(x, w):
    return x * lax.rsqrt(jnp.mean(x * x, axis=-1, keepdims=True) + EPS) * w


def _silu(x):
    return x * (1.0 / (1.0 + jnp.exp(-x)))


def _softplus(x):
    return jnp.maximum(x, 0.0) + jnp.log1p(jnp.exp(-jnp.abs(x)))


def _bdot(a, b):
    return jnp.dot(a.astype(BF16), b.astype(BF16), preferred_element_type=F32)


def _bdot_nt(a, b):
    return lax.dot_general(a.astype(BF16), b.astype(BF16), (((1,), (1,)), ((), ())), preferred_element_type=F32)


def _bdot_tn(a, b):
    return lax.dot_general(a.astype(BF16), b.astype(BF16), (((0,), (0,)), ((), ())), preferred_element_type=F32)


def _fdot(a, b):
    return jnp.dot(a, b, preferred_element_type=F32, precision=HIGHEST)


def _norm_proj_kernel(x_ref, nw_ref, *refs, transposed, splits):
    n_w = len(transposed)
    xn = _rms(x_ref[...], nw_ref[...]).astype(BF16)
    out_refs = iter(refs[n_w:])
    for w_ref, tr, widths in zip(refs[:n_w], transposed, splits):
        if tr:
            res = lax.dot_general(w_ref[...], xn, (((1,), (1,)), ((), ())), preferred_element_type=F32)
        else:
            res = jnp.dot(xn, w_ref[...], preferred_element_type=F32)
        col = 0
        for n in widths:
            o_ref = next(out_refs)
            o_ref[...] = (res[col:col + n, :] if tr else res[:, col:col + n]).astype(o_ref.dtype)
            col += n


def norm_proj(h2, nw, weights, transposed=None, splits=None, tm=512):
    T, D = h2.shape
    transposed = tuple(transposed or (False,) * len(weights))
    splits = tuple(tuple(s) if s else (w.shape[1],) for w, s in zip(weights, splits or (None,) * len(weights)))
    ws = [(w.T if tr else w).astype(BF16) for w, tr in zip(weights, transposed)]
    out_specs, out_shape = [], []
    for tr, widths in zip(transposed, splits):
        for n in widths:
            out_specs.append(pl.BlockSpec((n, tm), lambda i: (0, i)) if tr else pl.BlockSpec((tm, n), lambda i: (i, 0)))
            out_shape.append(jax.ShapeDtypeStruct((n, T) if tr else (T, n), F32))
    return pl.pallas_call(
        functools.partial(_norm_proj_kernel, transposed=transposed, splits=splits),
        grid=(T // tm,),
        in_specs=[pl.BlockSpec((tm, D), lambda i: (i, 0)), _const_spec((1, D))] + [_const_spec(w.shape) for w in ws],
        out_specs=out_specs,
        out_shape=out_shape,
        compiler_params=_cparams("parallel"),
        name="norm_proj",
    )(h2, nw.reshape(1, D), *ws)


def _proj_residual_kernel(h_ref, *refs, n_in, transposed):
    acc = h_ref[...]
    for a_ref, w_ref in zip(refs[:n_in], refs[n_in:2 * n_in]):
        if transposed:
            acc = acc + _bdot_tn(a_ref[...], w_ref[...])
        else:
            acc = acc + jnp.dot(a_ref[...].astype(BF16), w_ref[...], preferred_element_type=F32)
    refs[2 * n_in][...] = acc


def proj_residual(h2, acts, weights, transposed=False, tm=512):
    T, D = h2.shape
    n_in = len(acts)
    if transposed:
        act_specs = [pl.BlockSpec((a.shape[0], tm), lambda i: (0, i)) for a in acts]
    else:
        act_specs = [pl.BlockSpec((tm, a.shape[1]), lambda i: (i, 0)) for a in acts]
    return pl.pallas_call(
        functools.partial(_proj_residual_kernel, n_in=n_in, transposed=transposed),
        grid=(T // tm,),
        in_specs=[pl.BlockSpec((tm, D), lambda i: (i, 0))] + act_specs
        + [_const_spec(w.shape) for w in weights],
        out_specs=pl.BlockSpec((tm, D), lambda i: (i, 0)),
        out_shape=jax.ShapeDtypeStruct((T, D), F32),
        compiler_params=_cparams("parallel"),
        name="proj_residual",
    )(h2, *acts, *weights)


def _ffn_kernel(h_ref, nw_ref, wup_ref, cw_ref, cb_ref, wd_ref, fw_ref, o_ref,
                xn_ref, acc_ref, act_ref, pg_ref, pu_ref, *, n_chunks, final_norm):
    tm = h_ref.shape[0]
    dff = wd_ref.shape[0]
    gate_cols = lambda c: slice(c * FF_CHUNK, (c + 1) * FF_CHUNK)
    up_cols = lambda c: slice(dff + c * FF_CHUNK, dff + (c + 1) * FF_CHUNK)

    @pl.when(pl.program_id(1) == 0)
    def _():
        pg_ref[...] = jnp.zeros_like(pg_ref)
        pu_ref[...] = jnp.zeros_like(pu_ref)

    x = h_ref[...]
    xn_ref[...] = _rms(x, nw_ref[...]).astype(BF16)
    acc_ref[...] = x
    row = lax.broadcasted_iota(jnp.int32, (tm, 1), 0)

    def conv(u, prev, w, b):
        m1 = jnp.where(row == 0, prev[7:8], pltpu.roll(u, 1, 0))
        m2 = jnp.where(row == 0, prev[6:7], jnp.where(row == 1, prev[7:8], pltpu.roll(u, 2, 0)))
        return u * w[2:3] + m1 * w[1:2] + m2 * w[0:1] + b

    def up(c):
        xn = xn_ref[...]
        return (jnp.dot(xn, wup_ref[:, gate_cols(c)], preferred_element_type=F32),
                jnp.dot(xn, wup_ref[:, up_cols(c)], preferred_element_type=F32))

    nxt = up(0)
    for c in range(n_chunks):
        ug, uu = nxt
        if c + 1 < n_chunks:
            nxt = up(c + 1)
        yg = conv(ug, pg_ref[c], cw_ref[:, gate_cols(c)], cb_ref[:, gate_cols(c)])
        yu = conv(uu, pu_ref[c], cw_ref[:, up_cols(c)], cb_ref[:, up_cols(c)])
        pg_ref[c] = ug[tm - 8:tm]
        pu_ref[c] = uu[tm - 8:tm]
        act_ref[:, gate_cols(c)] = (_silu(yg) * yu).astype(BF16)
        if (c + 1) % DOWN_GROUP == 0 or c + 1 == n_chunks:
            rows = slice((c // DOWN_GROUP) * DOWN_GROUP * FF_CHUNK, (c + 1) * FF_CHUNK)
            acc_ref[...] += jnp.dot(act_ref[:, rows], wd_ref[rows, :], preferred_element_type=F32)
    out = acc_ref[...]
    if final_norm:
        out = _rms(out, fw_ref[...])
    o_ref[...] = out


def conv_ffn(h3, nw, w_up, conv_w, conv_b, w_down, final_w, final_norm, tm=512):
    B, S, D = h3.shape
    dff = w_down.shape[0]
    nch = dff // FF_CHUNK
    consts = (w_up.astype(BF16), conv_w, conv_b.reshape(1, 2 * dff), w_down.astype(BF16))
    tile = pl.BlockSpec((None, tm, D), lambda b, s: (b, s, 0))
    return pl.pallas_call(
        functools.partial(_ffn_kernel, n_chunks=nch, final_norm=final_norm),
        grid=(B, S // tm),
        in_specs=[tile, _const_spec((1, D))] + [_const_spec(a.shape) for a in consts] + [_const_spec((1, D))],
        out_specs=tile,
        out_shape=jax.ShapeDtypeStruct((B, S, D), F32),
        scratch_shapes=[pltpu.VMEM((tm, D), BF16), pltpu.VMEM((tm, D), F32), pltpu.VMEM((tm, dff), BF16),
                        pltpu.VMEM((nch, 8, FF_CHUNK), F32), pltpu.VMEM((nch, 8, FF_CHUNK), F32)],
        compiler_params=_cparams("parallel", "arbitrary"),
        name="conv_ffn",
    )(h3, nw.reshape(1, D), *consts, final_w.reshape(1, D))


def _causal_conv(x, xbuf, w, ts, width):
    xbuf[8:8 + ts, :] = x
    y = x * w[width - 1:width]
    for k in range(1, width):
        y = y + xbuf[pl.ds(8 - k, ts), :] * w[width - 1 - k:width - k]
    xbuf[0:8, :] = x[ts - 8:ts]
    return y


def _gdn_kernel(qkv_ref, sm_ref, z_ref, cw_ref, alog_ref, dtb_ref, gn_ref, o_ref,
                xbuf, qn, kn, vn, gb, state, *, ts):
    nh, dk, C = GDN_HEADS, GDN_HEAD_DIM, CHUNK
    R = nh * C

    @pl.when(pl.program_id(1) == 0)
    def _():
        xbuf[0:8, :] = jnp.zeros((8, xbuf.shape[1]), F32)
        state[...] = jnp.zeros_like(state)

    y = _silu(_causal_conv(qkv_ref[...], xbuf, cw_ref[...], ts, SHORT_CONV))
    for h in range(nh):
        qh = y[:, h * dk:(h + 1) * dk]
        qn[h] = qh * (lax.rsqrt(jnp.sum(qh * qh, axis=-1, keepdims=True) + EPS) * dk ** -0.5)
        kh = y[:, GDN_WIDTH + h * dk:GDN_WIDTH + (h + 1) * dk]
        kn[h] = kh * lax.rsqrt(jnp.sum(kh * kh, axis=-1, keepdims=True) + EPS)
        vn[h] = y[:, 2 * GDN_WIDTH + h * dk:2 * GDN_WIDTH + (h + 1) * dk]
    sm = sm_ref[...]
    lane = lax.broadcasted_iota(jnp.int32, sm.shape, 1)
    beta = 1.0 / (1.0 + jnp.exp(-sm))
    g = -jnp.exp(alog_ref[...]) * _softplus(sm + dtb_ref[...])
    gb[...] = jnp.where(lane < nh, beta, g)

    ri = lax.broadcasted_iota(jnp.int32, (R, R), 0)
    ci = lax.broadcasted_iota(jnp.int32, (R, R), 1)
    same_head = (ri >> 6) == (ci >> 6)
    causal = same_head & (ri >= ci)
    strict = same_head & (ri > ci)
    eye = (ri == ci).astype(F32)
    l64 = (lax.broadcasted_iota(jnp.int32, (C, C), 0) >= lax.broadcasted_iota(jnp.int32, (C, C), 1)).astype(F32)
    gnw = gn_ref[...]

    def stack(ref, r0):
        return jnp.concatenate([ref[h, pl.ds(r0, C), :] for h in range(nh)], axis=0)

    chunks = range(ts // C)

    def intra(group):
        each = lambda fn: {c: fn(c) for c in group}
        gbc = each(lambda c: gb[c * C:(c + 1) * C, :])
        gam = each(lambda c: _fdot(l64, gbc[c]))
        q = each(lambda c: stack(qn, c * C))
        k = each(lambda c: stack(kn, c * C))
        v = each(lambda c: stack(vn, c * C))
        beta_col = each(lambda c: jnp.concatenate([gbc[c][:, h:h + 1] for h in range(nh)], axis=0))
        gam_col = each(lambda c: jnp.concatenate([gam[c][:, nh + h:nh + h + 1] for h in range(nh)], axis=0))
        decay = each(lambda c: jnp.exp(jnp.where(
            causal, gam_col[c] - jnp.sum(eye * gam_col[c], axis=0, keepdims=True), NEG_BIG)))
        kb = each(lambda c: k[c] * beta_col[c])
        p = each(lambda c: jnp.where(strict, -(_bdot_nt(kb[c], k[c]) * decay[c]), 0.0))
        t = each(lambda c: eye + p[c])
        for _ in range(5):
            p = each(lambda c: _bdot(p[c], p[c]))
            t = each(lambda c: t[c] + _bdot(t[c], p[c]))
        eg = each(lambda c: jnp.exp(gam_col[c]))
        sol = each(lambda c: _bdot(t[c], jnp.concatenate([v[c] * beta_col[c], kb[c] * eg[c]], axis=1)))
        qk = each(lambda c: _bdot_nt(q[c], k[c]) * decay[c])
        gl = each(lambda c: [gam[c][C - 1:C, nh + h:nh + h + 1] for h in range(nh)])
        k_dec = each(lambda c: k[c] * jnp.exp(
            jnp.concatenate([jnp.broadcast_to(gl[c][h], (C, 1)) for h in range(nh)], axis=0) - gam_col[c]))
        wq = each(lambda c: [jnp.concatenate([sol[c][h * C:(h + 1) * C, dk:], (q[c] * eg[c])[h * C:(h + 1) * C]],
                                             axis=0) for h in range(nh)])
        return sol, qk, gl, k_dec, wq

    sol, qk, gl, k_dec, wq = {}, {}, {}, {}, {}
    for c0 in range(0, ts // C, GDN_INTERLEAVE):
        for dst, src in zip((sol, qk, gl, k_dec, wq), intra(range(c0, min(c0 + GDN_INTERLEAVE, ts // C)))):
            dst.update(src)

    heads = range(nh)
    s_cur = [state[h] for h in heads]
    for c in chunks:
        ws = [_bdot(wq[c][h], s_cur[h]) for h in heads]
        u = [sol[c][h * C:(h + 1) * C, :dk] - ws[h][:C] for h in heads]
        o = [ws[h][C:] + _bdot(qk[c][h * C:(h + 1) * C, h * C:(h + 1) * C], u[h]) for h in heads]
        s_cur = [s_cur[h] * jnp.exp(gl[c][h]) + _bdot_tn(k_dec[c][h * C:(h + 1) * C], u[h]) for h in heads]
        for h in heads:
            zh = z_ref[c * C:(c + 1) * C, h * dk:(h + 1) * dk]
            o_ref[c * C:(c + 1) * C, h * dk:(h + 1) * dk] = _rms(o[h], gnw) * _silu(zh)
    for h in heads:
        state[h] = s_cur[h]


def gdn_heads(qkv, small, z_a, conv_w, a_log, dt_bias, gnorm, ts=256):
    B, S, _ = qkv.shape
    nh, dk = GDN_HEADS, GDN_HEAD_DIM
    alog_l = jnp.zeros((1, LANES), F32).at[0, nh:2 * nh].set(a_log)
    dtb_l = jnp.zeros((1, LANES), F32).at[0, nh:2 * nh].set(dt_bias)

    def tile(n):
        return pl.BlockSpec((None, ts, n), lambda b, s: (b, s, 0))

    return pl.pallas_call(
        functools.partial(_gdn_kernel, ts=ts),
        grid=(B, S // ts),
        in_specs=[tile(3 * GDN_WIDTH), tile(LANES), tile(GDN_WIDTH), _const_spec(conv_w.shape),
                  _const_spec((1, LANES)), _const_spec((1, LANES)), _const_spec((1, dk))],
        out_specs=tile(GDN_WIDTH),
        out_shape=jax.ShapeDtypeStruct((B, S, GDN_WIDTH), F32),
        scratch_shapes=[pltpu.VMEM((ts + 8, 3 * GDN_WIDTH), F32)]
        + [pltpu.VMEM((nh, ts, dk), F32)] * 3
        + [pltpu.VMEM((ts, LANES), F32), pltpu.VMEM((nh, dk, dk), F32)],
        compiler_params=_cparams("parallel", "arbitrary"),
        name="gdn_heads",
    )(qkv, small, z_a, conv_w, alog_l, dtb_l, gnorm.reshape(1, dk))


def _ssd_kernel(xbc_ref, sm_ref, z_ref, cw_ref, cb_ref, dtb_ref, alog_ref, d_ref, nw_ref, e_ref, o_ref,
                xbuf, xs_s, b_s, c_s, dt_s, hstate, *, ts):
    C, P, N = CHUNK, SSD_HEAD_DIM, SSD_STATE
    hpg = SSD_HEADS // SSD_GROUPS
    gw = hpg * P

    @pl.when(pl.program_id(1) == 0)
    def _():
        xbuf[0:8, :] = jnp.zeros((8, xbuf.shape[1]), F32)
        hstate[...] = jnp.zeros_like(hstate)

    y = _silu(_causal_conv(xbc_ref[...], xbuf, cw_ref[...], ts, SHORT_CONV) + cb_ref[...])
    xs_s[...] = y[:, :SSD_WIDTH]
    b_s[...] = y[:, SSD_WIDTH:SSD_WIDTH + SSD_GROUPS * N]
    c_s[...] = y[:, SSD_WIDTH + SSD_GROUPS * N:]
    sm = sm_ref[...]
    lane = lax.broadcasted_iota(jnp.int32, sm.shape, 1)
    dt = jnp.where((lane >= SSD_HEADS) & (lane < 2 * SSD_HEADS), _softplus(sm + dtb_ref[...]), 0.0)
    dt_s[...] = _fdot(dt, e_ref[...])
    a_l = -jnp.exp(alog_ref[...])
    d_l = d_ref[...]
    nw = nw_ref[...]

    li = lax.broadcasted_iota(jnp.int32, (C, C), 0)
    si = lax.broadcasted_iota(jnp.int32, (C, C), 1)
    causal = li >= si
    l64 = causal.astype(F32)
    eye = (li == si).astype(F32)

    chunks = range(ts // C)
    groups = range(SSD_GROUPS)
    each = lambda fn: [fn(c) for c in chunks]
    each_g = lambda fn: [[fn(c, g) for g in groups] for c in chunks]
    gl = lambda g: slice(g * gw, (g + 1) * gw)
    rows = lambda c: slice(c * C, (c + 1) * C)
    dtc = each(lambda c: dt_s[rows(c), :])
    xs = each(lambda c: xs_s[rows(c), :])
    xc = each(lambda c: xs[c] * dtc[c])
    acs = each(lambda c: _fdot(l64, dtc[c] * a_l))
    eacs = each(lambda c: jnp.exp(acs[c]))
    xd = each(lambda c: xc[c] * jnp.exp(acs[c][C - 1:C, :] - acs[c]))
    e_last = each(lambda c: jnp.exp(acs[c][C - 1:C, :]))
    bg = each_g(lambda c, g: b_s[rows(c), g * N:(g + 1) * N])
    cg = each_g(lambda c, g: c_s[rows(c), g * N:(g + 1) * N])
    cb = each_g(lambda c, g: _bdot_nt(cg[c][g], bg[c][g]))
    s_inc = each_g(lambda c, g: _bdot_tn(bg[c][g], xd[c][:, gl(g)]))

    def diag(c, g):
        yd = []
        for j in range(hpg):
            hl = slice((g * hpg + j) * P, (g * hpg + j + 1) * P)
            blk = acs[c][:, hl]
            row = jnp.sum(blk * eye, axis=0, keepdims=True)
            m = cb[c][g] * jnp.exp(jnp.where(causal, blk - row, NEG_BIG))
            yd.append(_bdot(m, xc[c][:, hl]))
        return jnp.concatenate(yd, axis=1)

    y_diag = each_g(diag)

    h_cur = [hstate[g] for g in groups]
    for c in chunks:
        y_off = [_bdot(cg[c][g], h_cur[g]) * eacs[c][:, gl(g)] for g in groups]
        h_cur = [h_cur[g] * e_last[c][:, gl(g)] + s_inc[c][g] for g in groups]
        yy = jnp.concatenate([y_diag[c][g] + y_off[g] for g in groups], axis=1) + d_l * xs[c]
        yy = yy * _silu(z_ref[rows(c), :])
        o_ref[rows(c), :] = jnp.concatenate([_rms(yy[:, gl(g)], nw[:, gl(g)]) for g in groups], axis=1)
    for g in groups:
        hstate[g] = h_cur[g]


def ssd_heads(xbc, small, z_b, conv_w, conv_b, a_log, dt_bias, d_skip, norm_w, ts=256):
    B, S, _ = xbc.shape
    nh, P = SSD_HEADS, SSD_HEAD_DIM
    dtb_l = jnp.zeros((1, LANES), F32).at[0, nh:2 * nh].set(dt_bias)
    expand = np.zeros((LANES, SSD_WIDTH), np.float32)
    for h in range(nh):
        expand[nh + h, h * P:(h + 1) * P] = 1.0

    def tile(n):
        return pl.BlockSpec((None, ts, n), lambda b, s: (b, s, 0))

    gw = SSD_WIDTH // SSD_GROUPS
    return pl.pallas_call(
        functools.partial(_ssd_kernel, ts=ts),
        grid=(B, S // ts),
        in_specs=[tile(SSD_XBC), tile(LANES), tile(SSD_WIDTH), _const_spec(conv_w.shape), _const_spec((1, SSD_XBC)),
                  _const_spec((1, LANES)), _const_spec((1, SSD_WIDTH)), _const_spec((1, SSD_WIDTH)),
                  _const_spec((1, SSD_WIDTH)), _const_spec((LANES, SSD_WIDTH))],
        out_specs=tile(SSD_WIDTH),
        out_shape=jax.ShapeDtypeStruct((B, S, SSD_WIDTH), F32),
        scratch_shapes=[pltpu.VMEM((ts + 8, SSD_XBC), F32), pltpu.VMEM((ts, SSD_WIDTH), F32),
                        pltpu.VMEM((ts, SSD_GROUPS * SSD_STATE), F32), pltpu.VMEM((ts, SSD_GROUPS * SSD_STATE), F32),
                        pltpu.VMEM((ts, SSD_WIDTH), F32), pltpu.VMEM((SSD_GROUPS, SSD_STATE, gw), F32)],
        compiler_params=_cparams("parallel", "arbitrary"),
        name="ssd_heads",
    )(xbc, small, z_b, conv_w, conv_b.reshape(1, SSD_XBC), dtb_l, jnp.repeat(a_log, P).reshape(1, SSD_WIDTH),
      jnp.repeat(d_skip, P).reshape(1, SSD_WIDTH), norm_w.reshape(1, SSD_WIDTH), jnp.asarray(expand))


def hybrid_layer(h, nw, w_in, gdn_conv, gdn_a_log, gdn_dt_bias, gdn_norm, ssd_conv, ssd_conv_bias, ssd_a_log,
                 ssd_dt_bias, ssd_d, ssd_norm, w_out):
    B, S, D = h.shape
    T = B * S
    gwid, swid = GDN_WIDTH, SSD_WIDTH
    o_za = 3 * gwid
    o_ba = o_za + gwid
    o_zb = o_ba + 2 * GDN_HEADS
    o_xbc = o_zb + swid
    o_dt = o_xbc + SSD_XBC
    w_small = jnp.zeros((D, LANES), F32)
    w_small = w_small.at[:, :2 * GDN_HEADS].set(w_in[:, o_ba:o_zb])
    w_small = w_small.at[:, SSD_HEADS:2 * SSD_HEADS].set(w_in[:, o_dt:o_dt + SSD_HEADS])
    ws = [w_in[:, :o_za], w_in[:, o_za:o_ba], w_small, w_in[:, o_zb:o_xbc], w_in[:, o_xbc:o_dt]]
    h2 = h.reshape(T, D)
    qkv, z_a, small, z_b, xbc = norm_proj(h2, nw, [w.astype(BF16) for w in ws])
    small3 = small.reshape(B, S, LANES)
    o_a = gdn_heads(qkv.reshape(B, S, -1), small3, z_a.reshape(B, S, -1), gdn_conv, gdn_a_log, gdn_dt_bias, gdn_norm)
    y_b = ssd_heads(xbc.reshape(B, S, -1), small3, z_b.reshape(B, S, -1), ssd_conv, ssd_conv_bias, ssd_a_log,
                    ssd_dt_bias, ssd_d, ssd_norm)
    out = proj_residual(h2, [o_a.reshape(T, gwid), y_b.reshape(T, swid)],
                        [w_out[:gwid].astype(BF16), w_out[gwid:].astype(BF16)])
    return out.reshape(B, S, D)


def _compress_kernel(kc0_ref, kc1_ref, vc0_ref, vc1_ref, pk0_ref, pk1_ref, pv0_ref, pv1_ref,
                     w1ka_ref, w1kb_ref, w1va_ref, w1vb_ref, w2k_ref, w2v_ref, ko_ref, vo_ref):
    n_seg = kc0_ref.shape[0]
    for srcs, p0, p1, wa, wb, w2, out in (((kc0_ref, kc1_ref), pk0_ref, pk1_ref, w1ka_ref, w1kb_ref, w2k_ref, ko_ref),
                                          ((vc0_ref, vc1_ref), pv0_ref, pv1_ref, w1va_ref, w1vb_ref, w2v_ref, vo_ref)):
        for g in range(NSA_KV_GROUPS):
            t = srcs[g][...]
            f0 = _bdot(t + p0[...], wa[...])
            f1 = _bdot(t + p1[...], wb[...])
            hid = f0 + pltpu.roll(f1, n_seg - 1, 0)
            out[g] = _bdot(_silu(hid), w2[...])


def nsa_compress(kc_groups, vc_groups, pos_k, w1_k, w2_k, pos_v, w1_v, w2_v):
    B, n_seg, seg_w = kc_groups[0].shape
    G = NSA_KV_GROUPS
    half = CMP_STRIDE
    args = [*kc_groups, *vc_groups,
            pos_k[:half].reshape(1, seg_w), pos_k[half:].reshape(1, seg_w),
            pos_v[:half].reshape(1, seg_w), pos_v[half:].reshape(1, seg_w),
            w1_k[:seg_w].astype(BF16), w1_k[seg_w:].astype(BF16), w1_v[:seg_w].astype(BF16), w1_v[seg_w:].astype(BF16),
            w2_k.astype(BF16), w2_v.astype(BF16)]
    per_b = pl.BlockSpec((None, n_seg, seg_w), lambda b: (b, 0, 0))
    out_b = pl.BlockSpec((None, G, n_seg, NSA_HEAD_DIM), lambda b: (b, 0, 0, 0))
    return pl.pallas_call(
        _compress_kernel,
        grid=(B,),
        in_specs=[per_b] * (2 * G) + [_const_spec(a.shape) for a in args[2 * G:]],
        out_specs=[out_b, out_b],
        out_shape=[jax.ShapeDtypeStruct((B, G, n_seg, NSA_HEAD_DIM), F32)] * 2,
        compiler_params=_cparams("parallel"),
        name="nsa_compress",
    )(*args)


def _alibi_slopes():
    return [[float(2.0 ** (-8.0 * (g * NSA_J + j + 1) / NSA_HEADS)) for j in range(NSA_J)]
            for g in range(NSA_KV_GROUPS)]


M_INIT = -1e20


TQ = LANES
BF16_SUBLANES = 16
AUG = 128
POS_ROW0 = NSA_HEAD_DIM
SEL_ROW0 = POS_ROW0 + BF16_SUBLANES
ZERO_ROWS = BF16_SUBLANES
V_ROWS = NSA_HEAD_DIM + BF16_SUBLANES
POS_BITS = 6
N_SPLIT = 5
SEL_SHIFT = SEL_BLOCK.bit_length() - 1
LOG2E = math.log2(math.e)


def _slope_rows():
    G, J = NSA_KV_GROUPS, NSA_J
    rows = np.zeros((G, SEL_ROW0 - POS_ROW0, J * TQ), np.float32)
    for g, per_group in enumerate(_alibi_slopes()):
        for j, slope in enumerate(per_group):
            rest = float(np.float32(slope)) * LOG2E
            for i in range(N_SPLIT):
                piece = float(np.asarray(rest, dtype=BF16))
                rows[g, i, j * TQ:(j + 1) * TQ] = piece * 2.0 ** POS_BITS
                rows[g, N_SPLIT + i, j * TQ:(j + 1) * TQ] = piece
                rest -= piece
    return jnp.asarray(rows, BF16)


def _nsa_t_kernel(qT_ref, gT_ref, kc_ref, vc_ref, ks_ref, vsT_ref, kw_ref, vwT_ref, covT_ref, slope_ref, o_ref,
                  kaug_s, kaug_w, vaug_s, vaug_w, qaug, oc_ref, m_ref, acc_ref, *, tk, seq, n_cmp, n_sel):
    G, J, Dh, tq = NSA_KV_GROUPS, NSA_J, NSA_HEAD_DIM, TQ
    slopes = _alibi_slopes()
    qi = pl.program_id(1)
    start = qi * tq
    nc = kc_ref.shape[1]
    nsp = AUG - SEL_ROW0 - ZERO_ROWS
    top_k = min(SEL_TOPK, n_sel)
    lanes = J * tq

    @pl.when(qi == 0)
    def _():
        pos = lax.broadcasted_iota(jnp.int32, (seq, AUG - Dh), 0)
        col = lax.broadcasted_iota(jnp.int32, (seq, AUG - Dh), 1) + Dh
        pos_cols = jnp.where(col < POS_ROW0 + N_SPLIT, pos >> POS_BITS,
                             jnp.where(col < POS_ROW0 + 2 * N_SPLIT, pos & (2 ** POS_BITS - 1), 0)).astype(F32)
        onehot = ((col >= SEL_ROW0) & ((pos >> SEL_SHIFT) == col - SEL_ROW0)).astype(F32)
        ones_row = (lax.broadcasted_iota(jnp.int32, (V_ROWS - Dh, seq), 0) == 0).astype(F32)
        for g in range(G):
            for k_ref, ka, vT_ref, va, extra in ((ks_ref, kaug_s, vsT_ref, vaug_s, pos_cols + onehot),
                                                 (kw_ref, kaug_w, vwT_ref, vaug_w, pos_cols)):
                ka[g] = jnp.concatenate([k_ref[:, g * Dh:(g + 1) * Dh], extra], axis=1).astype(BF16)
                va[g] = jnp.concatenate([vT_ref[g * Dh:(g + 1) * Dh, :], ones_row], axis=0).astype(BF16)

    t_row = start + lax.broadcasted_iota(jnp.int32, (1, tq), 1)
    gates = 1.0 / (1.0 + jnp.exp(-gT_ref[...]))

    def per_head(x, fn):
        return jnp.concatenate([fn(j, x[:, j * tq:(j + 1) * tq]) for j in range(J)], axis=1)

    for g in range(G):
        qT = jnp.concatenate([qT_ref[(g * J + j) * Dh:(g * J + j + 1) * Dh, :] for j in range(J)], axis=1) * Dh ** -0.5

        n_idx = lax.broadcasted_iota(jnp.int32, (nc, tq), 0)
        dist_c = t_row - (n_idx * CMP_STRIDE + (CMP_BLOCK - 1))
        pen_c = jnp.where((dist_c >= 0) & (n_idx < n_cmp), 0.0, -NEG_BIG)
        distf_c = dist_c.astype(F32)
        s_c = per_head(_bdot(kc_ref[g], qT), lambda j, sj: sj - (slopes[g][j] * distf_c + pen_c))
        m_c = jnp.maximum(jnp.max(s_c, axis=0, keepdims=True), M_INIT)
        e_c = jnp.exp(s_c - m_c)
        p_c = e_c / jnp.maximum(jnp.sum(e_c, axis=0, keepdims=True), 1e-30)
        oc_ref[g] = _bdot_tn(vc_ref[g], p_c)
        p_sum = p_c[:, 0:tq]
        for j in range(1, J):
            p_sum = p_sum + p_c[:, j * tq:(j + 1) * tq]
        imp = _fdot(covT_ref[...], p_sum)

        blk = lax.broadcasted_iota(jnp.int32, (nsp, tq), 0)
        cur = t_row >> SEL_SHIFT
        forced = (blk == 0) | (blk == cur) | (blk == cur - 1)
        imp = jnp.where(forced, SEL_FORCE, jnp.where(blk * SEL_BLOCK <= t_row, imp, -SEL_FORCE))
        imp = jnp.where(blk < n_sel, imp, -3e38)
        sel_pen = jnp.full((nsp, tq), NEG_BIG, F32)
        for _ in range(top_k):
            mx = jnp.max(imp, axis=0, keepdims=True)
            first = jnp.min(jnp.where(imp == mx, blk, nsp), axis=0, keepdims=True)
            hit = blk == first
            sel_pen = jnp.where(hit, 0.0, sel_pen)
            imp = jnp.where(hit, -jnp.inf, imp)

        qaug[g] = jnp.concatenate(
            [(qT * LOG2E).astype(BF16), slope_ref[g], jnp.concatenate([sel_pen] * J, axis=1).astype(BF16),
             jnp.zeros((ZERO_ROWS, lanes), BF16)], axis=0)

    def attend(tile, k0_of, kaug, vaug):
        m_ref[...] = jnp.full(m_ref.shape, M_INIT, F32)
        acc_ref[...] = jnp.zeros(acc_ref.shape, F32)

        def scores(kt):
            k0 = pl.multiple_of(k0_of(kt), tile)
            return [jnp.dot(kaug[g, pl.ds(k0, tile), :], qaug[g], preferred_element_type=F32) for g in range(G)]

        def update(kt, sc, valid_fn):
            k0 = pl.multiple_of(k0_of(kt), tile)
            if valid_fn is not None:
                pos = k0 + lax.broadcasted_iota(jnp.int32, (tile, tq), 0)
                mask_pen = jnp.where(valid_fn(t_row - pos), 0.0, NEG_BIG)
            for g in range(G):
                s = sc[g]
                if valid_fn is not None:
                    s = per_head(s, lambda j, sj: sj + mask_pen)
                m_old = m_ref[g]
                m_new = jnp.maximum(m_old, jnp.max(s, axis=0, keepdims=True))
                e = jnp.exp2(s - m_new).astype(BF16)
                pv = jnp.dot(vaug[g, :, pl.ds(k0, tile)], e, preferred_element_type=F32)
                acc_ref[g] = jnp.exp2(m_old - m_new) * acc_ref[g] + pv
                m_ref[g] = m_new

        return scores, update

    def finish(g):
        acc = acc_ref[g]
        return acc[0:Dh] / jnp.maximum(acc[Dh:Dh + 1], 1e-30)

    n_past = (start + tq + tk - 1) // tk - 1
    sel_scores, sel_update = attend(tk, lambda kt: kt * tk, kaug_s, vaug_s)
    diag_valid = lambda dist: dist >= 0

    def past_pair(p, carry):
        s0 = sel_scores(2 * p)
        s1 = sel_scores(2 * p + 1)
        sel_update(2 * p, s0, None)
        sel_update(2 * p + 1, s1, None)
        return carry

    lax.fori_loop(0, n_past // 2, past_pair, 0)

    @pl.when(n_past % 2 == 1)
    def _():
        s0 = sel_scores(n_past - 1)
        s1 = sel_scores(n_past)
        sel_update(n_past - 1, s0, None)
        sel_update(n_past, s1, diag_valid)

    @pl.when(n_past % 2 == 0)
    def _():
        sel_update(n_past, sel_scores(n_past), diag_valid)

    o_s = [finish(g) for g in range(G)]

    band0 = jnp.clip(start - WINDOW, 0, seq - (WINDOW + tq))
    win_scores, win_update = attend(tq, lambda kt: band0 + kt * tq, kaug_w, vaug_w)
    n_win = (WINDOW + tq) // tq
    sc_next = win_scores(0)
    for kt in range(n_win):
        sc = sc_next
        if kt + 1 < n_win:
            sc_next = win_scores(kt + 1)
        win_update(kt, sc, lambda dist: (dist >= 0) & (dist < WINDOW))
    o_w = [finish(g) for g in range(G)]

    for g in range(G):
        def gate_row(branch):
            return jnp.concatenate([gates[branch * NSA_HEADS + g * J + j:branch * NSA_HEADS + g * J + j + 1, :]
                                    for j in range(J)], axis=1)

        out = gate_row(0) * oc_ref[g] + gate_row(1) * o_s[g] + gate_row(2) * o_w[g]
        for j in range(J):
            o_ref[(g * J + j) * Dh:(g * J + j + 1) * Dh, :] = out[:, j * tq:(j + 1) * tq].astype(o_ref.dtype)


def nsa_attention_t(qT, gT, k_cmp, v_cmp, k_s, vT_s, k_w, vT_w, B, S, tk=256):
    width, T = qT.shape
    tq = TQ
    n_seg = k_cmp.shape[2]
    n_cmp = (S - CMP_BLOCK) // CMP_STRIDE + 1
    n_sel = S // SEL_BLOCK
    nsp = AUG - SEL_ROW0 - ZERO_ROWS
    assert n_sel <= nsp and S >= WINDOW + tq and POS_ROW0 + 2 * N_SPLIT <= SEL_ROW0
    cs = np.arange(n_seg) * CMP_STRIDE
    ss = np.arange(n_sel) * SEL_BLOCK
    cover = np.clip(np.minimum(cs[:, None] + CMP_BLOCK, ss[None, :] + SEL_BLOCK)
                    - np.maximum(cs[:, None], ss[None, :]), 0, None) / CMP_STRIDE
    cover_t = np.zeros((nsp, n_seg), np.float32)
    cover_t[:n_sel, :n_cmp] = cover[:n_cmp].T
    slope_rows = _slope_rows()
    nq = S // tq

    def qtile(n):
        return pl.BlockSpec((n, tq), lambda b, s: (0, b * nq + s))

    per_b_cmp = pl.BlockSpec((None, NSA_KV_GROUPS, n_seg, NSA_HEAD_DIM), lambda b, s: (b, 0, 0, 0))
    per_b_k = pl.BlockSpec((S, NSA_KV), lambda b, s: (b, 0))
    per_b_vT = pl.BlockSpec((NSA_KV, S), lambda b, s: (0, b))
    G = NSA_KV_GROUPS
    return pl.pallas_call(
        functools.partial(_nsa_t_kernel, tk=tk, seq=S, n_cmp=n_cmp, n_sel=n_sel),
        grid=(B, nq),
        in_specs=[qtile(width), qtile(LANES), per_b_cmp, per_b_cmp, per_b_k, per_b_vT, per_b_k, per_b_vT,
                  _const_spec(cover_t.shape), _const_spec(slope_rows.shape)],
        out_specs=qtile(width),
        out_shape=jax.ShapeDtypeStruct((width, T), BF16),
        scratch_shapes=[pltpu.VMEM((G, S, AUG), BF16), pltpu.VMEM((G, S, AUG), BF16),
                        pltpu.VMEM((G, V_ROWS, S), BF16), pltpu.VMEM((G, V_ROWS, S), BF16),
                        pltpu.VMEM((G, AUG, NSA_J * tq), BF16), pltpu.VMEM((G, NSA_HEAD_DIM, NSA_J * tq), F32),
                        pltpu.VMEM((G, 1, NSA_J * tq), F32), pltpu.VMEM((G, V_ROWS, NSA_J * tq), F32)],
        compiler_params=_cparams("parallel", "arbitrary"),
        name="nsa_attention",
    )(qT, gT, k_cmp, v_cmp, k_s, vT_s, k_w, vT_w, jnp.asarray(cover_t), slope_rows)


def nsa_layer(h, nw, w_in, cmp_pos_k, cmp_w1_k, cmp_w2_k, cmp_pos_v, cmp_w1_v, cmp_w2_v, w_out):
    B, S, D = h.shape
    T = B * S
    width = NSA_HEADS * NSA_HEAD_DIM
    G, Dh = NSA_KV_GROUPS, NSA_HEAD_DIM
    w_gates = jnp.zeros((D, LANES), F32).at[:, :3 * NSA_HEADS].set(w_in[:, width + 6 * NSA_KV:])
    kv = [w_in[:, width + i * NSA_KV:width + (i + 1) * NSA_KV] for i in range(6)]
    h2 = h.reshape(T, D)
    ws = [w_in[:, :width], jnp.concatenate([kv[0], kv[1], kv[2], kv[4]], axis=1),
          jnp.concatenate([kv[3], kv[5], w_gates], axis=1)]
    outs = norm_proj(h2, nw, ws, transposed=(True, False, True),
                     splits=(None, (Dh,) * (2 * G) + (NSA_KV, NSA_KV), (NSA_KV, NSA_KV, LANES)))
    qT, (k_s, k_w, vT_s, vT_w, gT) = outs[0], outs[1 + 2 * G:]
    n_seg = S // CMP_STRIDE
    segments = lambda t: t.reshape(B, n_seg, CMP_STRIDE * Dh)
    k_cmp, v_cmp = nsa_compress([segments(t) for t in outs[1:1 + G]], [segments(t) for t in outs[1 + G:1 + 2 * G]],
                                cmp_pos_k, cmp_w1_k, cmp_w2_k, cmp_pos_v, cmp_w1_v, cmp_w2_v)
    oT = nsa_attention_t(qT, gT, k_cmp, v_cmp, k_s, vT_s, k_w, vT_w, B, S)
    out = proj_residual(h2, [oT], [w_out.astype(BF16)], transposed=True)
    return out.reshape(B, S, D)


def kernel(x, norm_mix, norm_ffn, norm_final, hy_w_in, gdn_conv, gdn_a_log, gdn_dt_bias, gdn_norm, ssd_conv, ssd_conv_bias, ssd_a_log, ssd_dt_bias, ssd_d, ssd_norm, hy_w_out, nsa_w_in, cmp_pos_k, cmp_w1_k, cmp_w2_k, cmp_pos_v, cmp_w1_v, cmp_w2_v, nsa_w_out, ffn_w_up, ffn_conv, ffn_conv_bias, ffn_w_down):
    depth = norm_mix.shape[0]
    h = x
    for layer in range(depth):
        e = layer // 2
        if layer % 2 == 0:
            h = hybrid_layer(h, norm_mix[layer], hy_w_in[e], gdn_conv[e], gdn_a_log[e], gdn_dt_bias[e], gdn_norm[e],
                             ssd_conv[e], ssd_conv_bias[e], ssd_a_log[e], ssd_dt_bias[e], ssd_d[e], ssd_norm[e],
                             hy_w_out[e])
        else:
            h = nsa_layer(h, norm_mix[layer], nsa_w_in[e], cmp_pos_k[e], cmp_w1_k[e], cmp_w2_k[e], cmp_pos_v[e],
                          cmp_w1_v[e], cmp_w2_v[e], nsa_w_out[e])
        h = conv_ffn(h, norm_ffn[layer], ffn_w_up[layer], ffn_conv[layer], ffn_conv_bias[layer], ffn_w_down[layer],
                     norm_final, layer == depth - 1)
    return h
```
